```python
import jax
import jax.numpy as jnp
from jax import lax
import numpy as np


D_MODEL = 2048
BATCH = 1
SEQ = 8192
DEPTH = 2

CTX_LEN = 256
GRID_W = 64
Q_BLOCK = 128
ROPE_BASE = 10000.0
EPS = 1e-6
N_BRANCH = 4
BRANCH_W = 1024
CONV_W = 1024
CONV_K = 3
GQA_HEADS = 8
GQA_KV_HEADS = 2
GQA_GROUP = GQA_HEADS // GQA_KV_HEADS
GQA_HEAD_DIM = 128
GQA_KV_W = GQA_KV_HEADS * GQA_HEAD_DIM
MLA_HEADS = 8
MLA_Q_RANK = 512
MLA_KV_RANK = 512
MLA_NOPE = 128
MLA_ROPE = 64
MLA_V = 128
SGU_W = 1024
SGU_GROUPS = 8
SGU_CHUNK = 128
D_FF = 4 * D_MODEL

KV_COLS = 2 * GQA_KV_W + MLA_KV_RANK + MLA_ROPE
Q_COLS = GQA_HEADS * GQA_HEAD_DIM + MLA_Q_RANK
A_COLS = 3 * CONV_W
D_COLS = 2 * SGU_W
G_COLS = N_BRANCH * D_MODEL
IN_COLS = KV_COLS + Q_COLS + A_COLS + D_COLS + G_COLS
IN_SPLITS = (KV_COLS, KV_COLS + Q_COLS, KV_COLS + Q_COLS + A_COLS, KV_COLS + Q_COLS + A_COLS + D_COLS)
KV_SPLITS = (GQA_KV_W, 2 * GQA_KV_W, 2 * GQA_KV_W + MLA_KV_RANK)
Q_SPLITS = (GQA_HEADS * GQA_HEAD_DIM,)

kernel_name = 'hybrid_parallel_dit_block'


def rms_norm(x, g):
    xf = x.astype(jnp.float32)
    y = xf * lax.rsqrt(jnp.mean(xf * xf, axis=-1, keepdims=True) + EPS)
    return (y * g.astype(jnp.float32)).astype(x.dtype)


def layer_norm(x, g, b):
    xf = x.astype(jnp.float32)
    mu = jnp.mean(xf, axis=-1, keepdims=True)
    xc = xf - mu
    y = xc * lax.rsqrt(jnp.mean(xc * xc, axis=-1, keepdims=True) + EPS)
    return (y * g.astype(jnp.float32) + b.astype(jnp.float32)).astype(x.dtype)


def modulate(x, g, shift, scale):
    return rms_norm(x, g) * (1 + scale) + shift


def adaln(cond, w, b):
    m = jax.nn.silu(cond) @ w + b
    if m.ndim == 2:
        m = m[:, None, :]
    return jnp.split(m, 6, axis=-1)


def axial_angles(n_tok, rot_dim):
    n_rows = n_tok // GRID_W
    row = jnp.repeat(jnp.arange(n_rows), GRID_W).astype(jnp.float32)
    col = jnp.tile(jnp.arange(GRID_W), n_rows).astype(jnp.float32)
    axis_dim = rot_dim // 2
    inv = ROPE_BASE ** (-jnp.arange(0, axis_dim, 2, dtype=jnp.float32) / axis_dim)
    return row[:, None] * inv, col[:, None] * inv


def rope_axis(x, ang):
    cos = jnp.cos(ang)[None, :, None, :].astype(x.dtype)
    sin = jnp.sin(ang)[None, :, None, :].astype(x.dtype)
    x1, x2 = jnp.split(x, 2, axis=-1)
    return jnp.concatenate([x1 * cos - x2 * sin, x1 * sin + x2 * cos], axis=-1)


def axial_rope(x, ang_row, ang_col):
    x_row, x_col = jnp.split(x, 2, axis=-1)
    return jnp.concatenate([rope_axis(x_row, ang_row), rope_axis(x_col, ang_col)], axis=-1)


def block_attention(q, k, v):
    B, S = q.shape[0], q.shape[1]
    n_blk = S // Q_BLOCK
    scale = q.shape[-1] ** -0.5
    qb = q.reshape((B, n_blk, Q_BLOCK) + q.shape[2:]).swapaxes(0, 1)

    def one_block(q_blk):
        s = jnp.einsum('bqhgd,bthd->bhgqt', q_blk, k).astype(jnp.float32) * scale
        p = jax.nn.softmax(s, axis=-1).astype(v.dtype)
        return jnp.einsum('bhgqt,bthe->bqhge', p, v)

    out = lax.map(one_block, qb)
    return out.swapaxes(0, 1).reshape((B, S) + out.shape[3:])


def short_conv(x, w):
    S = x.shape[1]
    pad = CONV_K // 2
    xp = jnp.pad(x, ((0, 0), (pad, pad), (0, 0)))
    out = xp[:, 0:S] * w[0]
    for tap in range(1, CONV_K):
        out = out + xp[:, tap:tap + S] * w[tap]
    return out


def spatial_gating(u, v, ln_g, ln_b, w_s, b_s):
    B, S, _ = v.shape
    n_chunk = S // SGU_CHUNK
    v = layer_norm(v, ln_g, ln_b)
    vc = v.reshape(B, n_chunk, SGU_CHUNK, SGU_GROUPS, SGU_W // SGU_GROUPS)
    mixed = jnp.einsum('gpq,bnqgc->bnpgc', w_s, vc) + b_s.T[:, :, None]
    return u * mixed.reshape(B, S, SGU_W)


def attn_kv(p_kv, lp, rope_b, rope_c):
    B, S, _ = p_kv.shape
    k_b, v_b, c_kv, k_r = jnp.split(p_kv, KV_SPLITS, axis=-1)
    k_b = rms_norm(k_b.reshape(B, S, GQA_KV_HEADS, GQA_HEAD_DIM), lp['k_norm_g'])
    v_b = v_b.reshape(B, S, GQA_KV_HEADS, GQA_HEAD_DIM)
    c_kv = rms_norm(c_kv, lp['mla_kv_norm_g'])
    kv = (c_kv @ lp['w_ukv']).reshape(B, S, MLA_HEADS, MLA_NOPE + MLA_V)
    k_nope, v_c = jnp.split(kv, (MLA_NOPE,), axis=-1)
    k_r = k_r[:, :, None, :]
    if rope_b is not None:
        k_b = axial_rope(k_b, *rope_b)
        k_r = axial_rope(k_r, *rope_c)
    k_c = jnp.concatenate([k_nope, jnp.broadcast_to(k_r, (B, S, MLA_HEADS, MLA_ROPE))], axis=-1)
    return k_b, v_b, k_c, v_c


def attn_q(p_q, lp, rope_b, rope_c):
    B, S, _ = p_q.shape
    q_b, c_q = jnp.split(p_q, Q_SPLITS, axis=-1)
    q_b = rms_norm(q_b.reshape(B, S, GQA_HEADS, GQA_HEAD_DIM), lp['q_norm_g'])
    c_q = rms_norm(c_q, lp['mla_q_norm_g'])
    q_c = (c_q @ lp['w_uq']).reshape(B, S, MLA_HEADS, MLA_NOPE + MLA_ROPE)
    q_nope, q_rope = jnp.split(q_c, (MLA_NOPE,), axis=-1)
    if rope_b is not None:
        q_b = axial_rope(q_b, *rope_b)
        q_rope = axial_rope(q_rope, *rope_c)
    q_b = q_b.reshape(B, S, GQA_KV_HEADS, GQA_GROUP, GQA_HEAD_DIM)
    q_c = jnp.concatenate([q_nope, q_rope], axis=-1)[:, :, :, None, :]
    return q_b, q_c


def token_mixer(h, lp, rope_b, rope_c, ctx_kv):
    B, S, _ = h.shape
    p = h @ lp['w_in']
    p_kv, p_q, p_a, p_d, p_g = jnp.split(p, IN_SPLITS, axis=-1)
    kv_self = attn_kv(p_kv, lp, rope_b, rope_c)
    if ctx_kv is None:
        k_b, v_b, k_c, v_c = kv_self
    else:
        k_b, v_b, k_c, v_c = [jnp.concatenate([kc, ks], axis=1) for kc, ks in zip(ctx_kv, kv_self)]
    q_b, q_c = attn_q(p_q, lp, rope_b, rope_c)
    y_b = block_attention(q_b, k_b, v_b).reshape(B, S, BRANCH_W)
    y_c = block_attention(q_c, k_c, v_c).reshape(B, S, BRANCH_W)
    gate_b, gate_c, x_a = jnp.split(p_a, 3, axis=-1)
    y_a = gate_b * short_conv(gate_c * x_a, lp['conv_w'])
    u, v = jnp.split(jax.nn.gelu(p_d), 2, axis=-1)
    y_d = spatial_gating(u, v, lp['sgu_ln_g'], lp['sgu_ln_b'], lp['sgu_w_s'], lp['sgu_b_s'])
    gates = jax.nn.sigmoid(p_g + lp['b_gate']).reshape(B, S, N_BRANCH, D_MODEL)
    ys = jnp.stack([y_a, y_b, y_c, y_d], axis=2)
    branch = jnp.einsum('bsie,ied->bsid', ys, lp['w_branch'])
    merged = jnp.sum(gates * branch, axis=2)
    return merged @ lp['w_out'], kv_self


def ffn(h, w1, w2):
    return jnp.square(jax.nn.relu(h @ w1)) @ w2


def setup_inputs(seed: int = 0) -> dict:
    key = jax.random.key(seed)
    ks = jax.random.split(key, 26)

    def nrm(i, shape, scale):
        return jax.random.normal(ks[i], shape, jnp.float32) * scale

    def gain(i, shape):
        return 1.0 + 0.02 * jax.random.normal(ks[i], shape, jnp.float32)

    L = DEPTH
    return {
        'x': nrm(0, (BATCH, SEQ, D_MODEL), 1.0),
        'c': nrm(1, (BATCH, D_MODEL), 1.0),
        'ctx': nrm(2, (BATCH, CTX_LEN, D_MODEL), 1.0),
        'c_ctx': nrm(3, (D_MODEL,), 1.0),
        'w_ada': nrm(4, (L, D_MODEL, 6 * D_MODEL), D_MODEL ** -0.5),
        'b_ada': nrm(5, (L, 6 * D_MODEL), 0.02),
        'norm_mix_g': gain(6, (L, D_MODEL)),
        'w_in': nrm(7, (L, D_MODEL, IN_COLS), D_MODEL ** -0.5),
        'b_gate': nrm(8, (L, G_COLS), 0.02),
        'conv_w': nrm(9, (L, CONV_K, CONV_W), CONV_K ** -0.5),
        'q_norm_g': gain(10, (L, GQA_HEAD_DIM)),
        'k_norm_g': gain(11, (L, GQA_HEAD_DIM)),
        'mla_q_norm_g': gain(12, (L, MLA_Q_RANK)),
        'mla_kv_norm_g': gain(13, (L, MLA_KV_RANK)),
        'w_uq': nrm(14, (L, MLA_Q_RANK, MLA_HEADS * (MLA_NOPE + MLA_ROPE)), MLA_Q_RANK ** -0.5),
        'w_ukv': nrm(15, (L, MLA_KV_RANK, MLA_HEADS * (MLA_NOPE + MLA_V)), MLA_KV_RANK ** -0.5),
        'sgu_ln_g': gain(16, (L, SGU_W)),
        'sgu_ln_b': nrm(17, (L, SGU_W), 0.02),
        'sgu_w_s': nrm(18, (L, SGU_GROUPS, SGU_CHUNK, SGU_CHUNK), SGU_CHUNK ** -0.5),
        'sgu_b_s': nrm(19, (L, SGU_GROUPS, SGU_CHUNK), 0.02),
        'w_branch': nrm(20, (L, N_BRANCH, BRANCH_W, D_MODEL), BRANCH_W ** -0.5),
        'w_out': nrm(21, (L, D_MODEL, D_MODEL), D_MODEL ** -0.5),
        'norm_ffn_g': gain(22, (L, D_MODEL)),
        'w_ff1': nrm(23, (L, D_MODEL, D_FF), D_MODEL ** -0.5),
        'w_ff2': nrm(24, (L, D_FF, D_MODEL), D_FF ** -0.5),
        'final_norm_g': gain(25, (D_MODEL,)),
    }


def reference(x, c, ctx, c_ctx, w_ada, b_ada, norm_mix_g, w_in, b_gate, conv_w, q_norm_g, k_norm_g,
              mla_q_norm_g, mla_kv_norm_g, w_uq, w_ukv, sgu_ln_g, sgu_ln_b, sgu_w_s, sgu_b_s,
              w_branch, w_out, norm_ffn_g, w_ff1, w_ff2, final_norm_g):
    S = x.shape[1]
    rope_b = axial_angles(S, GQA_HEAD_DIM)
    rope_c = axial_angles(S, MLA_ROPE)
    z = ctx
    for l in range(DEPTH):
        lp = {
            'w_in': w_in[l], 'b_gate': b_gate[l], 'conv_w': conv_w[l],
            'q_norm_g': q_norm_g[l], 'k_norm_g': k_norm_g[l],
            'mla_q_norm_g': mla_q_norm_g[l], 'mla_kv_norm_g': mla_kv_norm_g[l],
            'w_uq': w_uq[l], 'w_ukv': w_ukv[l],
            'sgu_ln_g': sgu_ln_g[l], 'sgu_ln_b': sgu_ln_b[l], 'sgu_w_s': sgu_w_s[l], 'sgu_b_s': sgu_b_s[l],
            'w_branch': w_branch[l], 'w_out': w_out[l],
        }
        sh1, sc1, g1, sh2, sc2, g2 = adaln(c, w_ada[l], b_ada[l])
        csh1, csc1, cg1, csh2, csc2, cg2 = adaln(c_ctx, w_ada[l], b_ada[l])
        hz = modulate(z, norm_mix_g[l], csh1, csc1)
        last = l == DEPTH - 1
        if last:
            ctx_kv = attn_kv(hz @ w_in[l][:, :KV_COLS], lp, None, None)
        else:
            mz, ctx_kv = token_mixer(hz, lp, None, None, None)
        hx = modulate(x, norm_mix_g[l], sh1, sc1)
        mx, _ = token_mixer(hx, lp, rope_b, rope_c, ctx_kv)
        x = x + g1 * mx
        x = x + g2 * ffn(modulate(x, norm_ffn_g[l], sh2, sc2), w_ff1[l], w_ff2[l])
        if not last:
            z = z + cg1 * mz
            z = z + cg2 * ffn(modulate(z, norm_ffn_g[l], csh2, csc2), w_ff1[l], w_ff2[l])
    return rms_norm(x, final_norm_g)
```

```python
import functools
import math

import jax
import jax.numpy as jnp
from jax import lax
from jax.experimental import pallas as pl
from jax.experimental.pallas import tpu as pltpu

F32 = jnp.float32
BF16 = jnp.bfloat16

D_MODEL = 2048
DEPTH = 2
GRID_W = 64
ROPE_BASE = 10000.0
EPS = 1e-6
N_BRANCH = 4
BRANCH_W = 1024
CONV_W = 1024
CONV_K = 3
GQA_HEADS = 8
GQA_KV_HEADS = 2
GQA_GROUP = GQA_HEADS // GQA_KV_HEADS
GQA_HEAD_DIM = 128
GQA_KV_W = GQA_KV_HEADS * GQA_HEAD_DIM
MLA_HEADS = 8
MLA_Q_RANK = 512
MLA_KV_RANK = 512
MLA_NOPE = 128
MLA_ROPE = 64
MLA_V = 128
SGU_W = 1024
SGU_GROUPS = 8
SGU_CHUNK = 128
D_FF = 4 * D_MODEL

KV_COLS = 2 * GQA_KV_W + MLA_KV_RANK + MLA_ROPE
Q_COLS = GQA_HEADS * GQA_HEAD_DIM + MLA_Q_RANK
A_COLS = 3 * CONV_W
D_COLS = 2 * SGU_W
G_COLS = N_BRANCH * D_MODEL

LANES = 128
BF16_SUBLANES = 16
MLA_QK_PAD = 256
VMEM_LIMIT_BYTES = 48 * 1024 * 1024

P_G = 0
P_A = P_G + G_COLS
P_D = P_A + A_COLS
P_QB = P_D + D_COLS
P_CQ = P_QB + GQA_HEADS * GQA_HEAD_DIM
P_KB = P_CQ + MLA_Q_RANK
P_VB = P_KB + GQA_KV_W
P_CKV = P_VB + GQA_KV_W
P_KR = P_CKV + MLA_KV_RANK
P_USED = P_KR + LANES
P_COLS = 16384
P_KV_START = P_KB - (P_KB % 512)


def _params(*sem):
    return pltpu.CompilerParams(dimension_semantics=sem, vmem_limit_bytes=VMEM_LIMIT_BYTES)


def _adaln_kernel(ct_ref, w_ref, b_ref, o_ref):
    ct = ct_ref[...]
    s = ct / (1.0 + jnp.exp(-ct))
    w = w_ref[...]
    for r in range(2):
        o_ref[r:r + 1, :] = jnp.sum(w * s[:, r:r + 1], axis=0, keepdims=True) + b_ref[...]


def adaln(cond_t, w, b, tn=512):
    d, n = w.shape
    return pl.pallas_call(
        _adaln_kernel,
        grid=(n // tn,),
        in_specs=[pl.BlockSpec((d, 2), lambda j: (0, 0)),
                  pl.BlockSpec((d, tn), lambda j: (0, j)),
                  pl.BlockSpec((1, tn), lambda j: (0, j))],
        out_specs=pl.BlockSpec((2, tn), lambda j: (0, j)),
        out_shape=jax.ShapeDtypeStruct((2, n), F32),
        compiler_params=_params("arbitrary"),
        name="adaln",
    )(cond_t, w, b)


def _norm_kernel(x_ref, g_ref, *rest, row):
    o_ref = rest[-1]
    x = x_ref[...]
    y = x * lax.rsqrt(jnp.mean(x * x, axis=-1, keepdims=True) + EPS) * g_ref[...]
    if len(rest) == 3:
        sh_ref, sc_ref = rest[0], rest[1]
        y = y * (1.0 + sc_ref[row:row + 1, :]) + sh_ref[row:row + 1, :]
    o_ref[...] = y.astype(o_ref.dtype)


def norm_mod(x, g, mod=None, row=0, k_shift=0, k_scale=1, out_dtype=BF16, tm=512):
    m, d = x.shape
    tm = min(tm, m)
    in_specs = [pl.BlockSpec((tm, d), lambda i: (i, 0)), pl.BlockSpec((1, d), lambda i: (0, 0))]
    args = [x, g.reshape(1, d)]
    if mod is not None:
        in_specs += [pl.BlockSpec((2, d), lambda i: (0, k_shift)), pl.BlockSpec((2, d), lambda i: (0, k_scale))]
        args += [mod, mod]
    return pl.pallas_call(
        functools.partial(_norm_kernel, row=row),
        grid=(m // tm,),
        in_specs=in_specs,
        out_specs=pl.BlockSpec((tm, d), lambda i: (i, 0)),
        out_shape=jax.ShapeDtypeStruct((m, d), out_dtype),
        compiler_params=_params("parallel"),
        name="norm_mod",
    )(*args)


def _mm_kernel(a_ref, w_ref, *rest, epilogue, row):
    o_ref = rest[-1]
    acc = jnp.dot(a_ref[...], w_ref[...], preferred_element_type=F32)
    if epilogue == "relu2":
        r = jnp.maximum(acc, 0.0)
        acc = r * r
    elif epilogue == "residual":
        res_ref, gate_ref = rest[0], rest[1]
        acc = res_ref[...] + gate_ref[row:row + 1, :] * acc
    o_ref[...] = acc.astype(o_ref.dtype)


def matmul(a, w, *, out_dtype=BF16, epilogue="none", res=None, mod=None, row=0, k_gate=0,
           tm=1024, tn=1024, name="matmul"):
    m, k = a.shape
    n = w.shape[1]
    tm, tn = min(tm, m), min(tn, n)
    assert m % tm == 0 and n % tn == 0
    in_specs = [pl.BlockSpec((tm, k), lambda i, j: (i, 0)), pl.BlockSpec((k, tn), lambda i, j: (0, j))]
    args = [a, w]
    if epilogue == "residual":
        kb = k_gate * (n // tn)
        in_specs += [pl.BlockSpec((tm, tn), lambda i, j: (i, j)),
                     pl.BlockSpec((2, tn), lambda i, j: (0, kb + j))]
        args += [res, mod]
    return pl.pallas_call(
        functools.partial(_mm_kernel, epilogue=epilogue, row=row),
        grid=(m // tm, n // tn),
        in_specs=in_specs,
        out_specs=pl.BlockSpec((tm, tn), lambda i, j: (i, j)),
        out_shape=jax.ShapeDtypeStruct((m, n), out_dtype),
        compiler_params=_params("parallel", "arbitrary"),
        name=name,
    )(*args)


def _rope(y, cos, sin, seg):
    lane = lax.broadcasted_iota(jnp.int32, y.shape, 1)
    first = (lane // seg) % 2 == 0
    partner = jnp.where(first, pltpu.roll(y, LANES - seg, 1), pltpu.roll(y, seg, 1))
    return y * cos + partner * sin


def _head_rms(x, g):
    return x * lax.rsqrt(jnp.mean(x * x, axis=-1, keepdims=True) + EPS) * g


GQA_SEG = GQA_HEAD_DIM // 4
MLA_SEG = MLA_ROPE // 4


def _qprep_kernel(qb_ref, cq_ref, qg_ref, cqg_ref, wuq_ref, *rest, rope):
    if rope:
        cb_ref, sb_ref, cc_ref, sc_ref, oqb_ref, oqc_ref = rest
    else:
        oqb_ref, oqc_ref = rest
    qg = qg_ref[...]
    scale_b = GQA_HEAD_DIM ** -0.5
    for h in range(GQA_HEADS):
        sl = slice(h * GQA_HEAD_DIM, (h + 1) * GQA_HEAD_DIM)
        y = _head_rms(qb_ref[:, sl].astype(F32), qg)
        if rope:
            y = _rope(y, cb_ref[...], sb_ref[...], GQA_SEG)
        oqb_ref[:, sl] = (y * scale_b).astype(oqb_ref.dtype)
    cq = _head_rms(cq_ref[...].astype(F32), cqg_ref[...]).astype(BF16)
    qc = jnp.dot(cq, wuq_ref[...], preferred_element_type=F32)
    scale_c = (MLA_NOPE + MLA_ROPE) ** -0.5
    for h in range(MLA_HEADS):
        nope = slice(h * MLA_QK_PAD, h * MLA_QK_PAD + MLA_NOPE)
        rot = slice(h * MLA_QK_PAD + MLA_NOPE, (h + 1) * MLA_QK_PAD)
        oqc_ref[:, nope] = (qc[:, nope] * scale_c).astype(oqc_ref.dtype)
        y = qc[:, rot]
        if rope:
            y = _rope(y, cc_ref[...], sc_ref[...], MLA_SEG)
        oqc_ref[:, rot] = (y * scale_c).astype(oqc_ref.dtype)


def qprep(p, off, qg, cqg, wuq, tables, tm=256):
    m = p.shape[0]
    tm = min(tm, m)
    wq = GQA_HEADS * GQA_HEAD_DIM
    qb_blk = (off + P_QB) // wq
    cq_blk = (off + P_CQ) // MLA_Q_RANK
    assert (off + P_QB) % wq == 0 and (off + P_CQ) % MLA_Q_RANK == 0
    in_specs = [pl.BlockSpec((tm, wq), lambda i: (i, qb_blk)),
                pl.BlockSpec((tm, MLA_Q_RANK), lambda i: (i, cq_blk)),
                pl.BlockSpec((1, GQA_HEAD_DIM), lambda i: (0, 0)),
                pl.BlockSpec((1, MLA_Q_RANK), lambda i: (0, 0)),
                pl.BlockSpec(wuq.shape, lambda i: (0, 0))]
    args = [p, p, qg.reshape(1, -1), cqg.reshape(1, -1), wuq]
    if tables is not None:
        in_specs += [pl.BlockSpec((tm, LANES), lambda i: (i, 0))] * 4
        args += list(tables)
    wc = MLA_HEADS * MLA_QK_PAD
    return pl.pallas_call(
        functools.partial(_qprep_kernel, rope=tables is not None),
        grid=(m // tm,),
        in_specs=in_specs,
        out_specs=[pl.BlockSpec((tm, wq), lambda i: (i, 0)), pl.BlockSpec((tm, wc), lambda i: (i, 0))],
        out_shape=[jax.ShapeDtypeStruct((m, wq), BF16), jax.ShapeDtypeStruct((m, wc), BF16)],
        compiler_params=_params("parallel"),
        name="qprep",
    )(*args)


def _kvprep_kernel(kb_ref, ckv_ref, kr_ref, kg_ref, ckvg_ref, wk_ref, wv_ref, *rest, rope):
    if rope:
        cb_ref, sb_ref, cc_ref, sc_ref, okb_ref, okc_ref, ovc_ref = rest
    else:
        okb_ref, okc_ref, ovc_ref = rest
    kg = kg_ref[...]
    for h in range(GQA_KV_HEADS):
        sl = slice(h * GQA_HEAD_DIM, (h + 1) * GQA_HEAD_DIM)
        y = _head_rms(kb_ref[:, sl].astype(F32), kg)
        if rope:
            y = _rope(y, cb_ref[...], sb_ref[...], GQA_SEG)
        okb_ref[:, sl] = y.astype(okb_ref.dtype)
    ckv = _head_rms(ckv_ref[...].astype(F32), ckvg_ref[...]).astype(BF16)
    knope = jnp.dot(ckv, wk_ref[...], preferred_element_type=F32)
    ovc_ref[...] = jnp.dot(ckv, wv_ref[...], preferred_element_type=F32).astype(ovc_ref.dtype)
    kr = kr_ref[...].astype(F32)
    if rope:
        kr = _rope(kr, cc_ref[...], sc_ref[...], MLA_SEG)
    kr = kr.astype(okc_ref.dtype)
    for h in range(MLA_HEADS):
        okc_ref[:, h * MLA_QK_PAD:h * MLA_QK_PAD + MLA_NOPE] = (
            knope[:, h * MLA_NOPE:(h + 1) * MLA_NOPE].astype(okc_ref.dtype))
        okc_ref[:, h * MLA_QK_PAD + MLA_NOPE:(h + 1) * MLA_QK_PAD] = kr


def kvprep(p, off, kg, ckvg, wk, wv, tables, tm=256):
    m = p.shape[0]
    tm = min(tm, m)
    kb_blk = (off + P_KB) // GQA_KV_W
    ckv_blk = (off + P_CKV) // MLA_KV_RANK
    kr_blk = (off + P_KR) // LANES
    assert (off + P_KB) % GQA_KV_W == 0 and (off + P_CKV) % MLA_KV_RANK == 0 and (off + P_KR) % LANES == 0
    in_specs = [pl.BlockSpec((tm, GQA_KV_W), lambda i: (i, kb_blk)),
                pl.BlockSpec((tm, MLA_KV_RANK), lambda i: (i, ckv_blk)),
                pl.BlockSpec((tm, LANES), lambda i: (i, kr_blk)),
                pl.BlockSpec((1, GQA_HEAD_DIM), lambda i: (0, 0)),
                pl.BlockSpec((1, MLA_KV_RANK), lambda i: (0, 0)),
                pl.BlockSpec(wk.shape, lambda i: (0, 0)),
                pl.BlockSpec(wv.shape, lambda i: (0, 0))]
    args = [p, p, p, kg.reshape(1, -1), ckvg.reshape(1, -1), wk, wv]
    if tables is not None:
        in_specs += [pl.BlockSpec((tm, LANES), lambda i: (i, 0))] * 4
        args += list(tables)
    wkc = MLA_HEADS * MLA_QK_PAD
    wvc = MLA_HEADS * MLA_V
    return pl.pallas_call(
        functools.partial(_kvprep_kernel, rope=tables is not None),
        grid=(m // tm,),
        in_specs=in_specs,
        out_specs=[pl.BlockSpec((tm, GQA_KV_W), lambda i: (i, 0)),
                   pl.BlockSpec((tm, wkc), lambda i: (i, 0)),
                   pl.BlockSpec((tm, wvc), lambda i: (i, 0))],
        out_shape=[jax.ShapeDtypeStruct((m, GQA_KV_W), BF16),
                   jax.ShapeDtypeStruct((m, wkc), BF16),
                   jax.ShapeDtypeStruct((m, wvc), BF16)],
        compiler_params=_params("parallel"),
        name="kvprep",
    )(*args)


def _attn_kernel(q_ref, k_ref, v_ref, o_ref, *, group, dq, dv, tk):
    tq = q_ref.shape[0]
    t_total = k_ref.shape[0]
    q = jnp.concatenate([q_ref[:, g * dq:(g + 1) * dq] for g in range(group)], axis=0)

    def scores(k):
        return lax.dot_general(q, k, (((1,), (1,)), ((), ())), preferred_element_type=F32)

    first = t_total % tk if t_total % tk else min(tk, t_total)
    s = scores(k_ref[0:first, :])
    m = jnp.max(s, axis=-1, keepdims=True)
    p = jnp.exp(s - m)
    l = jnp.sum(p, axis=-1, keepdims=True)
    acc = jnp.dot(p.astype(BF16), v_ref[0:first, :], preferred_element_type=F32)

    def body(j, carry):
        m, l, acc = carry
        start = pl.multiple_of(first + j * tk, BF16_SUBLANES)
        s = scores(k_ref[pl.ds(start, tk), :])
        m_new = jnp.maximum(m, jnp.max(s, axis=-1, keepdims=True))
        alpha = jnp.exp(m - m_new)
        p = jnp.exp(s - m_new)
        l = alpha * l + jnp.sum(p, axis=-1, keepdims=True)
        acc = alpha * acc + jnp.dot(p.astype(BF16), v_ref[pl.ds(start, tk), :], preferred_element_type=F32)
        return m_new, l, acc

    m, l, acc = lax.fori_loop(0, (t_total - first) // tk, body, (m, l, acc))
    out = acc / l
    for g in range(group):
        o_ref[:, g * dv:(g + 1) * dv] = out[g * tq:(g + 1) * tq, :].astype(o_ref.dtype)


def attention(q, k, v, *, n_kv, group, dq, dv, t_total, tq, tk=512, name="attn"):
    s_len = q.shape[0]
    tq = min(tq, s_len)
    return pl.pallas_call(
        functools.partial(_attn_kernel, group=group, dq=dq, dv=dv, tk=tk),
        grid=(n_kv, s_len // tq),
        in_specs=[pl.BlockSpec((tq, group * dq), lambda h, i: (i, h)),
                  pl.BlockSpec((t_total, dq), lambda h, i: (0, h)),
                  pl.BlockSpec((t_total, dv), lambda h, i: (0, h))],
        out_specs=pl.BlockSpec((tq, group * dv), lambda h, i: (i, h)),
        out_shape=jax.ShapeDtypeStruct((s_len, n_kv * group * dv), BF16),
        compiler_params=_params("parallel", "arbitrary"),
        name=name,
    )(q, k, v)


def _conv_kernel(gb_ref, gc_ref, xa_ref, gcp_ref, xap_ref, gcn_ref, xan_ref, w_ref, o_ref):
    i = pl.program_id(0)
    tm = gb_ref.shape[0]
    u = gc_ref[...].astype(F32) * xa_ref[...].astype(F32)
    halo = gcp_ref.shape[0]
    u_prev = gcp_ref[halo - 1:halo, :].astype(F32) * xap_ref[halo - 1:halo, :].astype(F32)
    u_next = gcn_ref[0:1, :].astype(F32) * xan_ref[0:1, :].astype(F32)
    u_prev = jnp.where(i == 0, 0.0, u_prev)
    u_next = jnp.where(i == pl.num_programs(0) - 1, 0.0, u_next)
    row = lax.broadcasted_iota(jnp.int32, u.shape, 0)
    below = jnp.where(row == 0, u_prev, pltpu.roll(u, 1, 0))
    above = jnp.where(row == tm - 1, u_next, pltpu.roll(u, tm - 1, 0))
    w = w_ref[...]
    y = below * w[0:1, :] + u * w[1:2, :] + above * w[2:3, :]
    o_ref[...] = (gb_ref[...].astype(F32) * y).astype(o_ref.dtype)


def short_conv(p, off, w, tm=512):
    m = p.shape[0]
    tm = min(tm, m)
    c = CONV_W
    blk = (off + P_A) // c
    assert (off + P_A) % c == 0
    hb = tm // BF16_SUBLANES
    n_hb = m // BF16_SUBLANES
    main = lambda k: pl.BlockSpec((tm, c), lambda i: (i, blk + k))
    prev = lambda k: pl.BlockSpec((BF16_SUBLANES, c), lambda i: (jnp.maximum(i * hb - 1, 0), blk + k))
    nxt = lambda k: pl.BlockSpec((BF16_SUBLANES, c), lambda i: (jnp.minimum((i + 1) * hb, n_hb - 1), blk + k))
    return pl.pallas_call(
        _conv_kernel,
        grid=(m // tm,),
        in_specs=[main(0), main(1), main(2), prev(1), prev(2), nxt(1), nxt(2),
                  pl.BlockSpec((CONV_K, c), lambda i: (0, 0))],
        out_specs=pl.BlockSpec((tm, c), lambda i: (i, 0)),
        out_shape=jax.ShapeDtypeStruct((m, c), BF16),
        compiler_params=_params("parallel"),
        name="short_conv",
    )(p, p, p, p, p, p, p, w)


def _gelu_tanh(x):
    c = math.sqrt(2.0 / math.pi)
    return x * (0.5 * (1.0 + jnp.tanh(c * (x + 0.044715 * (x * x * x)))))


def _sgu_kernel(u_ref, v_ref, g_ref, b_ref, ws_ref, bs_ref, o_ref):
    tm = u_ref.shape[0]
    n_chunk = tm // SGU_CHUNK
    gw = SGU_W // SGU_GROUPS
    v = _gelu_tanh(v_ref[...].astype(F32))
    mu = jnp.mean(v, axis=-1, keepdims=True)
    vc = v - mu
    vn = vc * lax.rsqrt(jnp.mean(vc * vc, axis=-1, keepdims=True) + EPS) * g_ref[...] + b_ref[...]
    vn = vn.astype(BF16)
    for g in range(SGU_GROUPS):
        cols = slice(g * gw, (g + 1) * gw)
        rhs = jnp.concatenate([vn[c * SGU_CHUNK:(c + 1) * SGU_CHUNK, cols] for c in range(n_chunk)], axis=1)
        mixed = jnp.dot(ws_ref[g], rhs, preferred_element_type=F32)
        for c in range(n_chunk):
            rows = slice(c * SGU_CHUNK, (c + 1) * SGU_CHUNK)
            u = _gelu_tanh(u_ref[rows, cols].astype(F32))
            o_ref[rows, cols] = (u * (mixed[:, c * gw:(c + 1) * gw] + bs_ref[g])).astype(o_ref.dtype)


def sgu(p, off, ln_g, ln_b, w_s, b_s, tm=512):
    m = p.shape[0]
    tm = min(tm, m)
    c = SGU_W
    blk = (off + P_D) // c
    assert (off + P_D) % c == 0
    gw = SGU_W // SGU_GROUPS
    bs = jnp.broadcast_to(b_s[:, :, None], (SGU_GROUPS, SGU_CHUNK, gw))
    return pl.pallas_call(
        _sgu_kernel,
        grid=(m // tm,),
        in_specs=[pl.BlockSpec((tm, c), lambda i: (i, blk)),
                  pl.BlockSpec((tm, c), lambda i: (i, blk + 1)),
                  pl.BlockSpec((1, c), lambda i: (0, 0)),
                  pl.BlockSpec((1, c), lambda i: (0, 0)),
                  pl.BlockSpec(w_s.shape, lambda i: (0, 0, 0)),
                  pl.BlockSpec(bs.shape, lambda i: (0, 0, 0))],
        out_specs=pl.BlockSpec((tm, c), lambda i: (i, 0)),
        out_shape=jax.ShapeDtypeStruct((m, c), BF16),
        compiler_params=_params("parallel"),
        name="sgu",
    )(p, p, ln_g.reshape(1, c), ln_b.reshape(1, c), w_s, bs)


def _merge_kernel(*refs):
    ys, pgs, bgs, wbs, o_ref = refs[0:4], refs[4:8], refs[8:12], refs[12:16], refs[16]
    acc = None
    for i in range(N_BRANCH):
        z = pgs[i][...].astype(F32) + bgs[i][...]
        gate = 1.0 / (1.0 + jnp.exp(-z))
        term = gate * jnp.dot(ys[i][...], wbs[i][...], preferred_element_type=F32)
        acc = term if acc is None else acc + term
    o_ref[...] = acc.astype(o_ref.dtype)


def merge(ys, p, off, b_gate, w_branch, tm=1024, tn=512):
    m = p.shape[0]
    tm = min(tm, m)
    nj = D_MODEL // tn
    g_blk = (off + P_G) // tn
    assert (off + P_G) % tn == 0
    in_specs = [pl.BlockSpec((tm, BRANCH_W), lambda i, j: (i, 0))] * N_BRANCH
    in_specs += [pl.BlockSpec((tm, tn), lambda i, j, b=b: (i, g_blk + b * nj + j)) for b in range(N_BRANCH)]
    in_specs += [pl.BlockSpec((1, tn), lambda i, j, b=b: (0, b * nj + j)) for b in range(N_BRANCH)]
    in_specs += [pl.BlockSpec((None, BRANCH_W, tn), lambda i, j, b=b: (b, 0, j)) for b in range(N_BRANCH)]
    return pl.pallas_call(
        _merge_kernel,
        grid=(m // tm, nj),
        in_specs=in_specs,
        out_specs=pl.BlockSpec((tm, tn), lambda i, j: (i, j)),
        out_shape=jax.ShapeDtypeStruct((m, D_MODEL), BF16),
        compiler_params=_params("parallel", "arbitrary"),
        name="merge",
    )(*ys, p, p, p, p, *([b_gate.reshape(1, -1)] * N_BRANCH), *([w_branch] * N_BRANCH))


def _rope_tables(n_tok):
    t = jnp.arange(n_tok)
    row = (t // GRID_W).astype(F32)[:, None]
    col = (t % GRID_W).astype(F32)[:, None]

    def table(rot_dim, pad):
        axis_dim = rot_dim // 2
        inv = ROPE_BASE ** (-jnp.arange(0, axis_dim, 2, dtype=F32) / axis_dim)
        ar, ac = row * inv, col * inv
        cos = jnp.concatenate([jnp.cos(ar), jnp.cos(ar), jnp.cos(ac), jnp.cos(ac)], axis=1)
        sin = jnp.concatenate([-jnp.sin(ar), jnp.sin(ar), -jnp.sin(ac), jnp.sin(ac)], axis=1)
        if pad:
            cos = jnp.pad(cos, ((0, 0), (0, pad)))
            sin = jnp.pad(sin, ((0, 0), (0, pad)))
        return cos, sin

    cb, sb = table(GQA_HEAD_DIM, 0)
    cc, sc = table(MLA_ROPE, LANES - MLA_ROPE)
    return cb, sb, cc, sc


def _layer_weights(w_in, w_uq, w_ukv, w_branch, w_out, w_ff1, w_ff2):
    kv, q, a, d = KV_COLS, KV_COLS + Q_COLS, KV_COLS + Q_COLS + A_COLS, KV_COLS + Q_COLS + A_COLS + D_COLS
    zeros = lambda n: jnp.zeros((D_MODEL, n), w_in.dtype)
    w_in_p = jnp.concatenate(
        [w_in[:, d:], w_in[:, q:a], w_in[:, a:d], w_in[:, kv:q], w_in[:, :kv],
         zeros(P_COLS - (G_COLS + A_COLS + D_COLS + Q_COLS + KV_COLS))], axis=1).astype(BF16)
    wuq = w_uq.reshape(MLA_Q_RANK, MLA_HEADS, MLA_NOPE + MLA_ROPE)
    wuq = jnp.pad(wuq, ((0, 0), (0, 0), (0, MLA_QK_PAD - MLA_NOPE - MLA_ROPE)))
    wuq = wuq.reshape(MLA_Q_RANK, MLA_HEADS * MLA_QK_PAD).astype(BF16)
    wukv = w_ukv.reshape(MLA_KV_RANK, MLA_HEADS, MLA_NOPE + MLA_V)
    wk = wukv[:, :, :MLA_NOPE].reshape(MLA_KV_RANK, MLA_HEADS * MLA_NOPE).astype(BF16)
    wv = wukv[:, :, MLA_NOPE:].reshape(MLA_KV_RANK, MLA_HEADS * MLA_V).astype(BF16)
    return dict(w_in=w_in_p, wuq=wuq, wk=wk, wv=wv, w_branch=w_branch.astype(BF16),
                w_out=w_out.astype(BF16), w_ff1=w_ff1.astype(BF16), w_ff2=w_ff2.astype(BF16))


def _kv_of(p, off, lw, lp, tables):
    kb, kc, vc = kvprep(p, off, lp["k_norm_g"], lp["mla_kv_norm_g"], lw["wk"], lw["wv"], tables)
    vb = p[:, off + P_VB:off + P_VB + GQA_KV_W]
    return kb, vb, kc, vc


def _mix(p, lw, lp, tables, kv):
    kb, vb, kc, vc = kv
    t_total = kb.shape[0]
    qb, qc = qprep(p, 0, lp["q_norm_g"], lp["mla_q_norm_g"], lw["wuq"], tables)
    y_b = attention(qb, kb, vb, n_kv=GQA_KV_HEADS, group=GQA_GROUP, dq=GQA_HEAD_DIM, dv=GQA_HEAD_DIM,
                    t_total=t_total, tq=128, name="attn_gqa")
    y_c = attention(qc, kc, vc, n_kv=MLA_HEADS, group=1, dq=MLA_QK_PAD, dv=MLA_V,
                    t_total=t_total, tq=512, name="attn_mla")
    y_a = short_conv(p, 0, lp["conv_w"])
    y_d = sgu(p, 0, lp["sgu_ln_g"], lp["sgu_ln_b"], lw["sgu_w_s"], lp["sgu_b_s"])
    return merge([y_a, y_b, y_c, y_d], p, 0, lp["b_gate"], lw["w_branch"])


def kernel(x, c, ctx, c_ctx, w_ada, b_ada, norm_mix_g, w_in, b_gate, conv_w, q_norm_g, k_norm_g,
           mla_q_norm_g, mla_kv_norm_g, w_uq, w_ukv, sgu_ln_g, sgu_ln_b, sgu_w_s, sgu_b_s,
           w_branch, w_out, norm_ffn_g, w_ff1, w_ff2, final_norm_g):
    assert x.shape[0] == 1 and c.shape[0] == 1 and ctx.shape[0] == 1
    xs, zs = x[0], ctx[0]
    tables = _rope_tables(xs.shape[0])
    cond_t = jnp.stack([c[0], c_ctx], axis=1)
    X, Z = 0, 1

    def ffn(s, mod, row, lw, g):
        h = norm_mod(s, g, mod, row=row, k_shift=3, k_scale=4)
        u = matmul(h, lw["w_ff1"], epilogue="relu2", name="ffn_up")
        return matmul(u, lw["w_ff2"], out_dtype=F32, epilogue="residual", res=s, mod=mod, row=row,
                      k_gate=5, tm=512, tn=512, name="ffn_down")

    for l in range(DEPTH):
        last = l == DEPTH - 1
        lw = _layer_weights(w_in[l], w_uq[l], w_ukv[l], w_branch[l], w_out[l], w_ff1[l], w_ff2[l])
        lw["sgu_w_s"] = sgu_w_s[l].astype(BF16)
        lp = dict(b_gate=b_gate[l], conv_w=conv_w[l], q_norm_g=q_norm_g[l], k_norm_g=k_norm_g[l],
                  mla_q_norm_g=mla_q_norm_g[l], mla_kv_norm_g=mla_kv_norm_g[l],
                  sgu_ln_g=sgu_ln_g[l], sgu_ln_b=sgu_ln_b[l], sgu_b_s=sgu_b_s[l])
        mod = adaln(cond_t, w_ada[l], b_ada[l].reshape(1, -1))

        hz = norm_mod(zs, norm_mix_g[l], mod, row=Z, k_shift=0, k_scale=1)
        if last:
            pz = matmul(hz, lw["w_in"][:, P_KV_START:], tn=512, name="in_proj_ctx_kv")
            kv_z = _kv_of(pz, -P_KV_START, lw, lp, None)
        else:
            pz = matmul(hz, lw["w_in"], name="in_proj")
            kv_z = _kv_of(pz, 0, lw, lp, None)

        hx = norm_mod(xs, norm_mix_g[l], mod, row=X, k_shift=0, k_scale=1)
        px = matmul(hx, lw["w_in"], name="in_proj")
        kv_x = _kv_of(px, 0, lw, lp, tables)
        kv = [jnp.concatenate([a, b], axis=0) for a, b in zip(kv_z, kv_x)]
        merged = _mix(px, lw, lp, tables, kv)
        xs = matmul(merged, lw["w_out"], out_dtype=F32, epilogue="residual", res=xs, mod=mod, row=X,
                    k_gate=2, name="out_proj")
        xs = ffn(xs, mod, X, lw, norm_ffn_g[l])

        if not last:
            merged_z = _mix(pz, lw, lp, None, kv_z)
            zs = matmul(merged_z, lw["w_out"], out_dtype=F32, epilogue="residual", res=zs, mod=mod, row=Z,
                        k_gate=2, name="out_proj")
            zs = ffn(zs, mod, Z, lw, norm_ffn_g[l])

    out = norm_mod(xs, final_norm_g, out_dtype=F32)
    return out[None]
```

```python
import functools
import math

import jax
import jax.numpy as jnp
from jax import lax
from jax.experimental import pallas as pl
from jax.experimental.pallas import tpu as pltpu

F32 = jnp.float32
BF16 = jnp.bfloat16

D_MODEL = 2048
DEPTH = 2
GRID_W = 64
ROPE_BASE = 10000.0
EPS = 1e-6
N_BRANCH = 4
BRANCH_W = 1024
CONV_W = 1024
CONV_K = 3
GQA_HEADS = 8
GQA_KV_HEADS = 2
GQA_GROUP = GQA_HEADS // GQA_KV_HEADS
GQA_HEAD_DIM = 128
GQA_KV_W = GQA_KV_HEADS * GQA_HEAD_DIM
MLA_HEADS = 8
MLA_Q_RANK = 512
MLA_KV_RANK = 512
MLA_NOPE = 128
MLA_ROPE = 64
MLA_V = 128
SGU_W = 1024
SGU_GROUPS = 8
SGU_CHUNK = 128
D_FF = 4 * D_MODEL

KV_COLS = 2 * GQA_KV_W + MLA_KV_RANK + MLA_ROPE
Q_COLS = GQA_HEADS * GQA_HEAD_DIM + MLA_Q_RANK
A_COLS = 3 * CONV_W
D_COLS = 2 * SGU_W
G_COLS = N_BRANCH * D_MODEL

LANES = 128
BF16_SUBLANES = 16
MLA_QK_PAD = 256
ATTN_TK = 512
VMEM_LIMIT_BYTES = 48 * 1024 * 1024

P_G = 0
P_A = P_G + G_COLS
P_D = P_A + A_COLS
P_QB = P_D + D_COLS
P_CQ = P_QB + GQA_HEADS * GQA_HEAD_DIM
P_KB = P_CQ + MLA_Q_RANK
P_VB = P_KB + GQA_KV_W
P_CKV = P_VB + GQA_KV_W
P_KR = P_CKV + MLA_KV_RANK
P_USED = P_KR + LANES
P_COLS = 16384
P_KV_START = P_KB - (P_KB % 512)


def _params(*sem):
    return pltpu.CompilerParams(dimension_semantics=sem, vmem_limit_bytes=VMEM_LIMIT_BYTES)


def _adaln_kernel(ct_ref, w_ref, b_ref, o_ref):
    ct = ct_ref[...]
    s = ct / (1.0 + jnp.exp(-ct))
    w = w_ref[...]
    for r in range(2):
        o_ref[r:r + 1, :] = jnp.sum(w * s[:, r:r + 1], axis=0, keepdims=True) + b_ref[...]


def adaln(cond_t, w, b, tn=512):
    d, n = w.shape
    return pl.pallas_call(
        _adaln_kernel,
        grid=(n // tn,),
        in_specs=[pl.BlockSpec((d, 2), lambda j: (0, 0)),
                  pl.BlockSpec((d, tn), lambda j: (0, j)),
                  pl.BlockSpec((1, tn), lambda j: (0, j))],
        out_specs=pl.BlockSpec((2, tn), lambda j: (0, j)),
        out_shape=jax.ShapeDtypeStruct((2, n), F32),
        compiler_params=_params("arbitrary"),
        name="adaln",
    )(cond_t, w, b)


def _norm_kernel(x_ref, g_ref, *rest, row):
    o_ref = rest[-1]
    x = x_ref[...]
    y = x * lax.rsqrt(jnp.mean(x * x, axis=-1, keepdims=True) + EPS) * g_ref[...]
    if len(rest) == 3:
        sh_ref, sc_ref = rest[0], rest[1]
        y = y * (1.0 + sc_ref[row:row + 1, :]) + sh_ref[row:row + 1, :]
    o_ref[...] = y.astype(o_ref.dtype)


def norm_mod(x, g, mod=None, row=0, k_shift=0, k_scale=1, out_dtype=BF16, tm=512):
    m, d = x.shape
    tm = min(tm, m)
    in_specs = [pl.BlockSpec((tm, d), lambda i: (i, 0)), pl.BlockSpec((1, d), lambda i: (0, 0))]
    args = [x, g.reshape(1, d)]
    if mod is not None:
        in_specs += [pl.BlockSpec((2, d), lambda i: (0, k_shift)), pl.BlockSpec((2, d), lambda i: (0, k_scale))]
        args += [mod, mod]
    return pl.pallas_call(
        functools.partial(_norm_kernel, row=row),
        grid=(m // tm,),
        in_specs=in_specs,
        out_specs=pl.BlockSpec((tm, d), lambda i: (i, 0)),
        out_shape=jax.ShapeDtypeStruct((m, d), out_dtype),
        compiler_params=_params("parallel"),
        name="norm_mod",
    )(*args)


def _mm_kernel(a_ref, w_ref, *rest, epilogue, row):
    o_ref = rest[-1]
    acc = jnp.dot(a_ref[...], w_ref[...], preferred_element_type=F32)
    if epilogue == "relu2":
        r = jnp.maximum(acc, 0.0)
        acc = r * r
    elif epilogue == "residual":
        res_ref, gate_ref = rest[0], rest[1]
        acc = res_ref[...] + gate_ref[row:row + 1, :] * acc
    o_ref[...] = acc.astype(o_ref.dtype)


def matmul(a, w, *, out_dtype=BF16, epilogue="none", res=None, mod=None, row=0, k_gate=0,
           tm=1024, tn=1024, name="matmul"):
    m, k = a.shape
    n = w.shape[1]
    tm, tn = min(tm, m), min(tn, n)
    assert m % tm == 0 and n % tn == 0
    in_specs = [pl.BlockSpec((tm, k), lambda i, j: (i, 0)), pl.BlockSpec((k, tn), lambda i, j: (0, j))]
    args = [a, w]
    if epilogue == "residual":
        kb = k_gate * (n // tn)
        in_specs += [pl.BlockSpec((tm, tn), lambda i, j: (i, j)),
                     pl.BlockSpec((2, tn), lambda i, j: (0, kb + j))]
        args += [res, mod]
    return pl.pallas_call(
        functools.partial(_mm_kernel, epilogue=epilogue, row=row),
        grid=(m // tm, n // tn),
        in_specs=in_specs,
        out_specs=pl.BlockSpec((tm, tn), lambda i, j: (i, j)),
        out_shape=jax.ShapeDtypeStruct((m, n), out_dtype),
        compiler_params=_params("parallel", "arbitrary"),
        name=name,
    )(*args)


def _rope(y, cos, sin, seg):
    lane = lax.broadcasted_iota(jnp.int32, y.shape, 1)
    first = (lane // seg) % 2 == 0
    partner = jnp.where(first, pltpu.roll(y, LANES - seg, 1), pltpu.roll(y, seg, 1))
    return y * cos + partner * sin


def _head_rms(x, g):
    return x * lax.rsqrt(jnp.mean(x * x, axis=-1, keepdims=True) + EPS) * g


LOG2_E = math.log2(math.e)
GQA_SEG = GQA_HEAD_DIM // 4
MLA_SEG = MLA_ROPE // 4


def _qprep_kernel(qb_ref, cq_ref, qg_ref, cqg_ref, wuq_ref, *rest, rope):
    if rope:
        cb_ref, sb_ref, cc_ref, sc_ref, oqb_ref, oqc_ref = rest
    else:
        oqb_ref, oqc_ref = rest
    qg = qg_ref[...]
    scale_b = LOG2_E * GQA_HEAD_DIM ** -0.5
    for h in range(GQA_HEADS):
        sl = slice(h * GQA_HEAD_DIM, (h + 1) * GQA_HEAD_DIM)
        y = _head_rms(qb_ref[:, sl].astype(F32), qg)
        if rope:
            y = _rope(y, cb_ref[...], sb_ref[...], GQA_SEG)
        oqb_ref[sl, :] = (y * scale_b).T.astype(oqb_ref.dtype)
    cq = _head_rms(cq_ref[...].astype(F32), cqg_ref[...]).astype(BF16)
    qc = jnp.dot(cq, wuq_ref[...], preferred_element_type=F32)
    scale_c = LOG2_E * (MLA_NOPE + MLA_ROPE) ** -0.5
    for h in range(MLA_HEADS):
        nope = slice(h * MLA_QK_PAD, h * MLA_QK_PAD + MLA_NOPE)
        rot = slice(h * MLA_QK_PAD + MLA_NOPE, (h + 1) * MLA_QK_PAD)
        oqc_ref[nope, :] = (qc[:, nope] * scale_c).T.astype(oqc_ref.dtype)
        y = qc[:, rot]
        if rope:
            y = _rope(y, cc_ref[...], sc_ref[...], MLA_SEG)
        oqc_ref[rot, :] = (y * scale_c).T.astype(oqc_ref.dtype)


def qprep(p, off, qg, cqg, wuq, tables, tm=256):
    m = p.shape[0]
    tm = min(tm, m)
    wq = GQA_HEADS * GQA_HEAD_DIM
    qb_blk = (off + P_QB) // wq
    cq_blk = (off + P_CQ) // MLA_Q_RANK
    assert (off + P_QB) % wq == 0 and (off + P_CQ) % MLA_Q_RANK == 0
    in_specs = [pl.BlockSpec((tm, wq), lambda i: (i, qb_blk)),
                pl.BlockSpec((tm, MLA_Q_RANK), lambda i: (i, cq_blk)),
                pl.BlockSpec((1, GQA_HEAD_DIM), lambda i: (0, 0)),
                pl.BlockSpec((1, MLA_Q_RANK), lambda i: (0, 0)),
                pl.BlockSpec(wuq.shape, lambda i: (0, 0))]
    args = [p, p, qg.reshape(1, -1), cqg.reshape(1, -1), wuq]
    if tables is not None:
        in_specs += [pl.BlockSpec((tm, LANES), lambda i: (i, 0))] * 4
        args += list(tables)
    wc = MLA_HEADS * MLA_QK_PAD
    return pl.pallas_call(
        functools.partial(_qprep_kernel, rope=tables is not None),
        grid=(m // tm,),
        in_specs=in_specs,
        out_specs=[pl.BlockSpec((wq, tm), lambda i: (0, i)), pl.BlockSpec((wc, tm), lambda i: (0, i))],
        out_shape=[jax.ShapeDtypeStruct((wq, m), BF16), jax.ShapeDtypeStruct((wc, m), BF16)],
        compiler_params=_params("parallel"),
        name="qprep",
    )(*args)


def _kvprep_kernel(kb_ref, vb_ref, ckv_ref, kr_ref, kg_ref, ckvg_ref, wk_ref, wv_ref, *rest, rope):
    if rope:
        cb_ref, sb_ref, cc_ref, sc_ref, okb_ref, okc_ref, ovb_ref, ovc_ref = rest
    else:
        okb_ref, okc_ref, ovb_ref, ovc_ref = rest
    kg = kg_ref[...]
    for h in range(GQA_KV_HEADS):
        sl = slice(h * GQA_HEAD_DIM, (h + 1) * GQA_HEAD_DIM)
        y = _head_rms(kb_ref[:, sl].astype(F32), kg)
        if rope:
            y = _rope(y, cb_ref[...], sb_ref[...], GQA_SEG)
        okb_ref[:, sl] = y.astype(okb_ref.dtype)
    ovb_ref[...] = vb_ref[...].astype(F32).T.astype(ovb_ref.dtype)
    ckv = _head_rms(ckv_ref[...].astype(F32), ckvg_ref[...]).astype(BF16)
    knope = jnp.dot(ckv, wk_ref[...], preferred_element_type=F32)
    ovc_ref[...] = jnp.dot(ckv, wv_ref[...], preferred_element_type=F32).T.astype(ovc_ref.dtype)
    kr = kr_ref[...].astype(F32)
    if rope:
        kr = _rope(kr, cc_ref[...], sc_ref[...], MLA_SEG)
    kr = kr.astype(okc_ref.dtype)
    for h in range(MLA_HEADS):
        okc_ref[:, h * MLA_QK_PAD:h * MLA_QK_PAD + MLA_NOPE] = (
            knope[:, h * MLA_NOPE:(h + 1) * MLA_NOPE].astype(okc_ref.dtype))
        okc_ref[:, h * MLA_QK_PAD + MLA_NOPE:(h + 1) * MLA_QK_PAD] = kr


def kvprep(p, off, kg, ckvg, wk, wv, tables):
    m = p.shape[0]
    tm = min(ATTN_TK, m)
    kb_blk = (off + P_KB) // GQA_KV_W
    vb_blk = (off + P_VB) // GQA_KV_W
    ckv_blk = (off + P_CKV) // MLA_KV_RANK
    kr_blk = (off + P_KR) // LANES
    assert (off + P_KB) % GQA_KV_W == 0 and (off + P_CKV) % MLA_KV_RANK == 0 and (off + P_KR) % LANES == 0
    in_specs = [pl.BlockSpec((tm, GQA_KV_W), lambda i: (i, kb_blk)),
                pl.BlockSpec((tm, GQA_KV_W), lambda i: (i, vb_blk)),
                pl.BlockSpec((tm, MLA_KV_RANK), lambda i: (i, ckv_blk)),
                pl.BlockSpec((tm, LANES), lambda i: (i, kr_blk)),
                pl.BlockSpec((1, GQA_HEAD_DIM), lambda i: (0, 0)),
                pl.BlockSpec((1, MLA_KV_RANK), lambda i: (0, 0)),
                pl.BlockSpec(wk.shape, lambda i: (0, 0)),
                pl.BlockSpec(wv.shape, lambda i: (0, 0))]
    args = [p, p, p, p, kg.reshape(1, -1), ckvg.reshape(1, -1), wk, wv]
    if tables is not None:
        in_specs += [pl.BlockSpec((tm, LANES), lambda i: (i, 0))] * 4
        args += list(tables)
    wkc = MLA_HEADS * MLA_QK_PAD
    wvc = MLA_HEADS * MLA_V
    n = m // tm
    return pl.pallas_call(
        functools.partial(_kvprep_kernel, rope=tables is not None),
        grid=(n,),
        in_specs=in_specs,
        out_specs=[pl.BlockSpec((tm, GQA_KV_W), lambda i: (i, 0)),
                   pl.BlockSpec((tm, wkc), lambda i: (i, 0)),
                   pl.BlockSpec((None, GQA_KV_W, tm), lambda i: (i, 0, 0)),
                   pl.BlockSpec((None, wvc, tm), lambda i: (i, 0, 0))],
        out_shape=[jax.ShapeDtypeStruct((m, GQA_KV_W), BF16),
                   jax.ShapeDtypeStruct((m, wkc), BF16),
                   jax.ShapeDtypeStruct((n, GQA_KV_W, tm), BF16),
                   jax.ShapeDtypeStruct((n, wvc, tm), BF16)],
        compiler_params=_params("parallel"),
        name="kvprep",
    )(*args)


def _attn_kernel(*refs, has_x):
    if has_x:
        qt_ref, kc_ref, vct_ref, kx_ref, vxt_ref, o_ref, s_ref = refs
    else:
        qt_ref, kc_ref, vct_ref, o_ref = refs
    qt = qt_ref[...]
    st = jnp.dot(kc_ref[...], qt, preferred_element_type=F32)
    m = jnp.max(st, axis=0, keepdims=True)
    pt = jnp.exp2(st - m)
    l = jnp.sum(pt, axis=0, keepdims=True)
    acc = jnp.dot(vct_ref[...], pt.astype(BF16), preferred_element_type=F32)

    if has_x:
        n, _, tk = vxt_ref.shape
        assert n % 2 == 0

        def qk(j, slot):
            k = kx_ref[pl.ds(pl.multiple_of(j * tk, tk), tk), :]
            st = jnp.dot(k, qt, preferred_element_type=F32)
            s_ref[slot] = st
            return jnp.max(st, axis=0, keepdims=True)

        def pv(j, slot, m_old, m_new, l, acc):
            pt = jnp.exp2(s_ref[slot] - m_new)
            alpha = jnp.exp2(m_old - m_new)
            l = alpha * l + jnp.sum(pt, axis=0, keepdims=True)
            acc = alpha * acc + jnp.dot(vxt_ref[j], pt.astype(BF16), preferred_element_type=F32)
            return l, acc

        m_old, m_new = m, jnp.maximum(m, qk(0, 0))
        for j in range(n - 1):
            m_next = jnp.maximum(m_new, qk(j + 1, (j + 1) % 2))
            l, acc = pv(j, j % 2, m_old, m_new, l, acc)
            m_old, m_new = m_new, m_next
        l, acc = pv(n - 1, (n - 1) % 2, m_old, m_new, l, acc)
    o_ref[...] = (acc / l).T.astype(o_ref.dtype)


def attention(qt, ctx_kv, x_kv, *, n_heads, group, dq, dv, tq=512, name="attn"):
    s_len = qt.shape[1]
    tq = min(tq, s_len)
    kc, vct = ctx_kv
    tc = kc.shape[0]
    in_specs = [pl.BlockSpec((dq, tq), lambda h, i: (h, i)),
                pl.BlockSpec((tc, dq), lambda h, i: (0, h // group)),
                pl.BlockSpec((None, dv, tc), lambda h, i: (0, h // group, 0))]
    args = [qt, kc, vct]
    scratch = []
    if x_kv is not None:
        kx, vxt = x_kv
        in_specs += [pl.BlockSpec((kx.shape[0], dq), lambda h, i: (0, h // group)),
                     pl.BlockSpec((vxt.shape[0], dv, vxt.shape[2]), lambda h, i: (0, h // group, 0))]
        args += [kx, vxt]
        scratch = [pltpu.VMEM((2, vxt.shape[2], tq), F32)]
    return pl.pallas_call(
        functools.partial(_attn_kernel, has_x=x_kv is not None),
        grid=(n_heads, s_len // tq),
        in_specs=in_specs,
        out_specs=pl.BlockSpec((tq, dv), lambda h, i: (i, h)),
        out_shape=jax.ShapeDtypeStruct((s_len, n_heads * dv), BF16),
        scratch_shapes=scratch,
        compiler_params=_params("parallel", "arbitrary"),
        name=name,
    )(*args)


def _conv_kernel(gb_ref, gc_ref, xa_ref, gcp_ref, xap_ref, gcn_ref, xan_ref, w_ref, o_ref):
    i = pl.program_id(0)
    tm = gb_ref.shape[0]
    u = gc_ref[...].astype(F32) * xa_ref[...].astype(F32)
    halo = gcp_ref.shape[0]
    u_prev = gcp_ref[halo - 1:halo, :].astype(F32) * xap_ref[halo - 1:halo, :].astype(F32)
    u_next = gcn_ref[0:1, :].astype(F32) * xan_ref[0:1, :].astype(F32)
    u_prev = jnp.where(i == 0, 0.0, u_prev)
    u_next = jnp.where(i == pl.num_programs(0) - 1, 0.0, u_next)
    row = lax.broadcasted_iota(jnp.int32, u.shape, 0)
    below = jnp.where(row == 0, u_prev, pltpu.roll(u, 1, 0))
    above = jnp.where(row == tm - 1, u_next, pltpu.roll(u, tm - 1, 0))
    w = w_ref[...]
    y = below * w[0:1, :] + u * w[1:2, :] + above * w[2:3, :]
    o_ref[...] = (gb_ref[...].astype(F32) * y).astype(o_ref.dtype)


def short_conv(p, off, w, tm=512):
    m = p.shape[0]
    tm = min(tm, m)
    c = CONV_W
    blk = (off + P_A) // c
    assert (off + P_A) % c == 0
    hb = tm // BF16_SUBLANES
    n_hb = m // BF16_SUBLANES
    main = lambda k: pl.BlockSpec((tm, c), lambda i: (i, blk + k))
    prev = lambda k: pl.BlockSpec((BF16_SUBLANES, c), lambda i: (jnp.maximum(i * hb - 1, 0), blk + k))
    nxt = lambda k: pl.BlockSpec((BF16_SUBLANES, c), lambda i: (jnp.minimum((i + 1) * hb, n_hb - 1), blk + k))
    return pl.pallas_call(
        _conv_kernel,
        grid=(m // tm,),
        in_specs=[main(0), main(1), main(2), prev(1), prev(2), nxt(1), nxt(2),
                  pl.BlockSpec((CONV_K, c), lambda i: (0, 0))],
        out_specs=pl.BlockSpec((tm, c), lambda i: (i, 0)),
        out_shape=jax.ShapeDtypeStruct((m, c), BF16),
        compiler_params=_params("parallel"),
        name="short_conv",
    )(p, p, p, p, p, p, p, w)


def _gelu_tanh(x):
    c = math.sqrt(2.0 / math.pi)
    return x * (0.5 * (1.0 + jnp.tanh(c * (x + 0.044715 * (x * x * x)))))


def _sgu_kernel(u_ref, v_ref, g_ref, b_ref, ws_ref, bs_ref, o_ref):
    tm = u_ref.shape[0]
    n_chunk = tm // SGU_CHUNK
    gw = SGU_W // SGU_GROUPS
    v = _gelu_tanh(v_ref[...].astype(F32))
    mu = jnp.mean(v, axis=-1, keepdims=True)
    vc = v - mu
    vn = vc * lax.rsqrt(jnp.mean(vc * vc, axis=-1, keepdims=True) + EPS) * g_ref[...] + b_ref[...]
    vn = vn.astype(BF16)
    for g in range(SGU_GROUPS):
        cols = slice(g * gw, (g + 1) * gw)
        rhs = jnp.concatenate([vn[c * SGU_CHUNK:(c + 1) * SGU_CHUNK, cols] for c in range(n_chunk)], axis=1)
        mixed = jnp.dot(ws_ref[g], rhs, preferred_element_type=F32)
        for c in range(n_chunk):
            rows = slice(c * SGU_CHUNK, (c + 1) * SGU_CHUNK)
            u = _gelu_tanh(u_ref[rows, cols].astype(F32))
            o_ref[rows, cols] = (u * (mixed[:, c * gw:(c + 1) * gw] + bs_ref[g])).astype(o_ref.dtype)


def sgu(p, off, ln_g, ln_b, w_s, b_s, tm=512):
    m = p.shape[0]
    tm = min(tm, m)
    c = SGU_W
    blk = (off + P_D) // c
    assert (off + P_D) % c == 0
    gw = SGU_W // SGU_GROUPS
    bs = jnp.broadcast_to(b_s[:, :, None], (SGU_GROUPS, SGU_CHUNK, gw))
    return pl.pallas_call(
        _sgu_kernel,
        grid=(m // tm,),
        in_specs=[pl.BlockSpec((tm, c), lambda i: (i, blk)),
                  pl.BlockSpec((tm, c), lambda i: (i, blk + 1)),
                  pl.BlockSpec((1, c), lambda i: (0, 0)),
                  pl.BlockSpec((1, c), lambda i: (0, 0)),
                  pl.BlockSpec(w_s.shape, lambda i: (0, 0, 0)),
                  pl.BlockSpec(bs.shape, lambda i: (0, 0, 0))],
        out_specs=pl.BlockSpec((tm, c), lambda i: (i, 0)),
        out_shape=jax.ShapeDtypeStruct((m, c), BF16),
        compiler_params=_params("parallel"),
        name="sgu",
    )(p, p, ln_g.reshape(1, c), ln_b.reshape(1, c), w_s, bs)


def _merge_kernel(*refs):
    ys, pgs, bgs, wbs, o_ref = refs[0:4], refs[4:8], refs[8:12], refs[12:16], refs[16]
    acc = None
    for i in range(N_BRANCH):
        z = pgs[i][...].astype(F32) + bgs[i][...]
        gate = 1.0 / (1.0 + jnp.exp(-z))
        term = gate * jnp.dot(ys[i][...], wbs[i][...], preferred_element_type=F32)
        acc = term if acc is None else acc + term
    o_ref[...] = acc.astype(o_ref.dtype)


def merge(ys, p, off, b_gate, w_branch, tm=1024, tn=512):
    m = p.shape[0]
    tm = min(tm, m)
    nj = D_MODEL // tn
    g_blk = (off + P_G) // tn
    assert (off + P_G) % tn == 0
    in_specs = [pl.BlockSpec((tm, BRANCH_W), lambda i, j: (i, 0))] * N_BRANCH
    in_specs += [pl.BlockSpec((tm, tn), lambda i, j, b=b: (i, g_blk + b * nj + j)) for b in range(N_BRANCH)]
    in_specs += [pl.BlockSpec((1, tn), lambda i, j, b=b: (0, b * nj + j)) for b in range(N_BRANCH)]
    in_specs += [pl.BlockSpec((None, BRANCH_W, tn), lambda i, j, b=b: (b, 0, j)) for b in range(N_BRANCH)]
    return pl.pallas_call(
        _merge_kernel,
        grid=(m // tm, nj),
        in_specs=in_specs,
        out_specs=pl.BlockSpec((tm, tn), lambda i, j: (i, j)),
        out_shape=jax.ShapeDtypeStruct((m, D_MODEL), BF16),
        compiler_params=_params("parallel", "arbitrary"),
        name="merge",
    )(*ys, p, p, p, p, *([b_gate.reshape(1, -1)] * N_BRANCH), *([w_branch] * N_BRANCH))


def _rope_tables(n_tok):
    t = jnp.arange(n_tok)
    row = (t // GRID_W).astype(F32)[:, None]
    col = (t % GRID_W).astype(F32)[:, None]

    def table(rot_dim, pad):
        axis_dim = rot_dim // 2
        inv = ROPE_BASE ** (-jnp.arange(0, axis_dim, 2, dtype=F32) / axis_dim)
        ar, ac = row * inv, col * inv
        cos = jnp.concatenate([jnp.cos(ar), jnp.cos(ar), jnp.cos(ac), jnp.cos(ac)], axis=1)
        sin = jnp.concatenate([-jnp.sin(ar), jnp.sin(ar), -jnp.sin(ac), jnp.sin(ac)], axis=1)
        if pad:
            cos = jnp.pad(cos, ((0, 0), (0, pad)))
            sin = jnp.pad(sin, ((0, 0), (0, pad)))
        return cos, sin

    cb, sb = table(GQA_HEAD_DIM, 0)
    cc, sc = table(MLA_ROPE, LANES - MLA_ROPE)
    return cb, sb, cc, sc


def _layer_weights(w_in, w_uq, w_ukv, w_branch, w_out, w_ff1, w_ff2):
    kv, q, a, d = KV_COLS, KV_COLS + Q_COLS, KV_COLS + Q_COLS + A_COLS, KV_COLS + Q_COLS + A_COLS + D_COLS
    zeros = lambda n: jnp.zeros((D_MODEL, n), w_in.dtype)
    w_in_p = jnp.concatenate(
        [w_in[:, d:], w_in[:, q:a], w_in[:, a:d], w_in[:, kv:q], w_in[:, :kv],
         zeros(P_COLS - (G_COLS + A_COLS + D_COLS + Q_COLS + KV_COLS))], axis=1).astype(BF16)
    wuq = w_uq.reshape(MLA_Q_RANK, MLA_HEADS, MLA_NOPE + MLA_ROPE)
    wuq = jnp.pad(wuq, ((0, 0), (0, 0), (0, MLA_QK_PAD - MLA_NOPE - MLA_ROPE)))
    wuq = wuq.reshape(MLA_Q_RANK, MLA_HEADS * MLA_QK_PAD).astype(BF16)
    wukv = w_ukv.reshape(MLA_KV_RANK, MLA_HEADS, MLA_NOPE + MLA_V)
    wk = wukv[:, :, :MLA_NOPE].reshape(MLA_KV_RANK, MLA_HEADS * MLA_NOPE).astype(BF16)
    wv = wukv[:, :, MLA_NOPE:].reshape(MLA_KV_RANK, MLA_HEADS * MLA_V).astype(BF16)
    return dict(w_in=w_in_p, wuq=wuq, wk=wk, wv=wv, w_branch=w_branch.astype(BF16),
                w_out=w_out.astype(BF16), w_ff1=w_ff1.astype(BF16), w_ff2=w_ff2.astype(BF16))


def _kv_of(p, off, lw, lp, tables):
    return kvprep(p, off, lp["k_norm_g"], lp["mla_kv_norm_g"], lw["wk"], lw["wv"], tables)


def _mix(p, lw, lp, tables, ctx_kv, x_kv):
    qbt, qct = qprep(p, 0, lp["q_norm_g"], lp["mla_q_norm_g"], lw["wuq"], tables)
    gqa = lambda kv: None if kv is None else (kv[0], kv[2])
    mla = lambda kv: None if kv is None else (kv[1], kv[3])
    y_b = attention(qbt, gqa(ctx_kv), gqa(x_kv), n_heads=GQA_HEADS, group=GQA_GROUP, dq=GQA_HEAD_DIM,
                    dv=GQA_HEAD_DIM, name="attn_gqa")
    y_c = attention(qct, mla(ctx_kv), mla(x_kv), n_heads=MLA_HEADS, group=1, dq=MLA_QK_PAD, dv=MLA_V,
                    name="attn_mla")
    y_a = short_conv(p, 0, lp["conv_w"])
    y_d = sgu(p, 0, lp["sgu_ln_g"], lp["sgu_ln_b"], lw["sgu_w_s"], lp["sgu_b_s"])
    return merge([y_a, y_b, y_c, y_d], p, 0, lp["b_gate"], lw["w_branch"])


def kernel(x, c, ctx, c_ctx, w_ada, b_ada, norm_mix_g, w_in, b_gate, conv_w, q_norm_g, k_norm_g,
           mla_q_norm_g, mla_kv_norm_g, w_uq, w_ukv, sgu_ln_g, sgu_ln_b, sgu_w_s, sgu_b_s,
           w_branch, w_out, norm_ffn_g, w_ff1, w_ff2, final_norm_g):
    assert x.shape[0] == 1 and c.shape[0] == 1 and ctx.shape[0] == 1
    xs, zs = x[0], ctx[0]
    tables = _rope_tables(xs.shape[0])
    cond_t = jnp.stack([c[0], c_ctx], axis=1)
    X, Z = 0, 1

    def ffn(s, mod, row, lw, g):
        h = norm_mod(s, g, mod, row=row, k_shift=3, k_scale=4)
        u = matmul(h, lw["w_ff1"], epilogue="relu2", name="ffn_up")
        return matmul(u, lw["w_ff2"], out_dtype=F32, epilogue="residual", res=s, mod=mod, row=row,
                      k_gate=5, tm=512, tn=512, name="ffn_down")

    for l in range(DEPTH):
        last = l == DEPTH - 1
        lw = _layer_weights(w_in[l], w_uq[l], w_ukv[l], w_branch[l], w_out[l], w_ff1[l], w_ff2[l])
        lw["sgu_w_s"] = sgu_w_s[l].astype(BF16)
        lp = dict(b_gate=b_gate[l], conv_w=conv_w[l], q_norm_g=q_norm_g[l], k_norm_g=k_norm_g[l],
                  mla_q_norm_g=mla_q_norm_g[l], mla_kv_norm_g=mla_kv_norm_g[l],
                  sgu_ln_g=sgu_ln_g[l], sgu_ln_b=sgu_ln_b[l], sgu_b_s=sgu_b_s[l])
        mod = adaln(cond_t, w_ada[l], b_ada[l].reshape(1, -1))

        hz = norm_mod(zs, norm_mix_g[l], mod, row=Z, k_shift=0, k_scale=1)
        if last:
            pz = matmul(hz, lw["w_in"][:, P_KV_START:], tn=512, name="in_proj_ctx_kv")
            kv_z = _kv_of(pz, -P_KV_START, lw, lp, None)
        else:
            pz = matmul(hz, lw["w_in"], name="in_proj")
            kv_z = _kv_of(pz, 0, lw, lp, None)

        hx = norm_mod(xs, norm_mix_g[l], mod, row=X, k_shift=0, k_scale=1)
        px = matmul(hx, lw["w_in"], name="in_proj")
        kv_x = _kv_of(px, 0, lw, lp, tables)
        merged = _mix(px, lw, lp, tables, kv_z, kv_x)
        xs = matmul(merged, lw["w_out"], out_dtype=F32, epilogue="residual", res=xs, mod=mod, row=X,
                    k_gate=2, name="out_proj")
        xs = ffn(xs, mod, X, lw, norm_ffn_g[l])

        if not last:
            merged_z = _mix(pz, lw, lp, None, kv_z, None)
            zs = matmul(merged_z, lw["w_out"], out_dtype=F32, epilogue="residual", res=zs, mod=mod, row=Z,
                        k_gate=2, name="out_proj")
            zs = ffn(zs, mod, Z, lw, norm_ffn_g[l])

    out = norm_mod(xs, final_norm_g, out_dtype=F32)
    return out[None]
```

```python
import functools
import math

import jax
import jax.numpy as jnp
from jax import lax
from jax.experimental import pallas as pl
from jax.experimental.pallas import tpu as pltpu

F32 = jnp.float32
BF16 = jnp.bfloat16

D_MODEL = 2048
DEPTH = 2
GRID_W = 64
ROPE_BASE = 10000.0
EPS = 1e-6
N_BRANCH = 4
BRANCH_W = 1024
CONV_W = 1024
CONV_K = 3
GQA_HEADS = 8
GQA_KV_HEADS = 2
GQA_GROUP = GQA_HEADS // GQA_KV_HEADS
GQA_HEAD_DIM = 128
GQA_KV_W = GQA_KV_HEADS * GQA_HEAD_DIM
MLA_HEADS = 8
MLA_Q_RANK = 512
MLA_KV_RANK = 512
MLA_NOPE = 128
MLA_ROPE = 64
MLA_V = 128
SGU_W = 1024
SGU_GROUPS = 8
SGU_CHUNK = 128
D_FF = 4 * D_MODEL

KV_COLS = 2 * GQA_KV_W + MLA_KV_RANK + MLA_ROPE
Q_COLS = GQA_HEADS * GQA_HEAD_DIM + MLA_Q_RANK
A_COLS = 3 * CONV_W
D_COLS = 2 * SGU_W
G_COLS = N_BRANCH * D_MODEL

LANES = 128
BF16_SUBLANES = 16
MLA_QK_PAD = 256
ATTN_TK = 512
VMEM_LIMIT_BYTES = 48 * 1024 * 1024

P_G = 0
P_A = P_G + G_COLS
P_D = P_A + A_COLS
P_QB = P_D + D_COLS
P_CQ = P_QB + GQA_HEADS * GQA_HEAD_DIM
P_KB = P_CQ + MLA_Q_RANK
P_VB = P_KB + GQA_KV_W
P_CKV = P_VB + GQA_KV_W
P_KR = P_CKV + MLA_KV_RANK
P_USED = P_KR + LANES
P_COLS = 16384
P_KV_START = P_KB - (P_KB % 512)


def _params(*sem):
    return pltpu.CompilerParams(dimension_semantics=sem, vmem_limit_bytes=VMEM_LIMIT_BYTES)


def _adaln_kernel(ct_ref, w_ref, b_ref, o_ref):
    ct = ct_ref[...]
    s = ct / (1.0 + jnp.exp(-ct))
    w = w_ref[...]
    for r in range(2):
        o_ref[r:r + 1, :] = jnp.sum(w * s[:, r:r + 1], axis=0, keepdims=True) + b_ref[...]


def adaln(cond_t, w, b, l, tn=512):
    _, d, n = w.shape
    return pl.pallas_call(
        _adaln_kernel,
        grid=(n // tn,),
        in_specs=[pl.BlockSpec((d, 2), lambda j: (0, 0)),
                  pl.BlockSpec((None, d, tn), lambda j: (l, 0, j)),
                  pl.BlockSpec((None, 1, tn), lambda j: (l, 0, j))],
        out_specs=pl.BlockSpec((2, tn), lambda j: (0, j)),
        out_shape=jax.ShapeDtypeStruct((2, n), F32),
        compiler_params=_params("arbitrary"),
        name="adaln",
    )(cond_t, w, b)


def _norm_kernel(x_ref, g_ref, *rest, row):
    o_ref = rest[-1]
    x = x_ref[...]
    y = x * lax.rsqrt(jnp.mean(x * x, axis=-1, keepdims=True) + EPS) * g_ref[...]
    if len(rest) == 3:
        sh_ref, sc_ref = rest[0], rest[1]
        y = y * (1.0 + sc_ref[row:row + 1, :]) + sh_ref[row:row + 1, :]
    o_ref[...] = y.astype(o_ref.dtype)


def norm_mod(x, g, mod=None, row=0, k_shift=0, k_scale=1, out_dtype=BF16, tm=512):
    m, d = x.shape
    tm = min(tm, m)
    in_specs = [pl.BlockSpec((tm, d), lambda i: (i, 0)), pl.BlockSpec((1, d), lambda i: (0, 0))]
    args = [x, g.reshape(1, d)]
    if mod is not None:
        in_specs += [pl.BlockSpec((2, d), lambda i: (0, k_shift)), pl.BlockSpec((2, d), lambda i: (0, k_scale))]
        args += [mod, mod]
    return pl.pallas_call(
        functools.partial(_norm_kernel, row=row),
        grid=(m // tm,),
        in_specs=in_specs,
        out_specs=pl.BlockSpec((tm, d), lambda i: (i, 0)),
        out_shape=jax.ShapeDtypeStruct((m, d), out_dtype),
        compiler_params=_params("parallel"),
        name="norm_mod",
    )(*args)


def _mm_kernel(a_ref, w_ref, *rest, epilogue, row, cast_w):
    if cast_w:
        wb_ref = rest[-1]
        rest = rest[:-1]

        @pl.when(pl.program_id(1) == 0)
        def _():
            wb_ref[...] = w_ref[...].astype(BF16)

        w = wb_ref[...]
    else:
        w = w_ref[...]
    o_ref = rest[-1]
    acc = jnp.dot(a_ref[...], w, preferred_element_type=F32)
    if epilogue == "relu2":
        r = jnp.maximum(acc, 0.0)
        acc = r * r
    elif epilogue == "residual":
        res_ref, gate_ref = rest[0], rest[1]
        acc = res_ref[...] + gate_ref[row:row + 1, :] * acc
    o_ref[...] = acc.astype(o_ref.dtype)


def matmul(a, w, l, *, n=None, col0=0, out_dtype=BF16, epilogue="none", res=None, mod=None, row=0, k_gate=0,
           tm=1024, tn=1024, name="matmul"):
    m, k = a.shape
    n = w.shape[2] if n is None else n
    tm, tn = min(tm, m), min(tn, n)
    assert m % tm == 0 and n % tn == 0
    cast_w = w.dtype == F32
    order = (lambda f: (lambda j, i: f(i, j))) if cast_w else (lambda f: f)
    in_specs = [pl.BlockSpec((tm, k), order(lambda i, j: (i, 0))),
                pl.BlockSpec((None, k, tn), order(lambda i, j: (l, 0, col0 + j)))]
    args = [a, w]
    if epilogue == "residual":
        kb = k_gate * (n // tn)
        in_specs += [pl.BlockSpec((tm, tn), order(lambda i, j: (i, j))),
                     pl.BlockSpec((2, tn), order(lambda i, j: (0, kb + j)))]
        args += [res, mod]
    grid = (n // tn, m // tm) if cast_w else (m // tm, n // tn)
    return pl.pallas_call(
        functools.partial(_mm_kernel, epilogue=epilogue, row=row, cast_w=cast_w),
        grid=grid,
        in_specs=in_specs,
        out_specs=pl.BlockSpec((tm, tn), order(lambda i, j: (i, j))),
        out_shape=jax.ShapeDtypeStruct((m, n), out_dtype),
        scratch_shapes=[pltpu.VMEM((k, tn), BF16)] if cast_w else [],
        compiler_params=_params("parallel", "arbitrary"),
        name=name,
    )(*args)


def _rope(y, cos, sin, seg):
    lane = lax.broadcasted_iota(jnp.int32, y.shape, 1)
    first = (lane // seg) % 2 == 0
    partner = jnp.where(first, pltpu.roll(y, LANES - seg, 1), pltpu.roll(y, seg, 1))
    return y * cos + partner * sin


def _head_rms(x, g):
    return x * lax.rsqrt(jnp.mean(x * x, axis=-1, keepdims=True) + EPS) * g


LOG2_E = math.log2(math.e)
GQA_SEG = GQA_HEAD_DIM // 4
MLA_SEG = MLA_ROPE // 4


def _qprep_kernel(qb_ref, cq_ref, qg_ref, cqg_ref, wuq_ref, *rest, rope):
    if rope:
        cb_ref, sb_ref, cc_ref, sc_ref, oqb_ref, oqc_ref = rest
    else:
        oqb_ref, oqc_ref = rest
    qg = qg_ref[...]
    scale_b = LOG2_E * GQA_HEAD_DIM ** -0.5
    for h in range(GQA_HEADS):
        sl = slice(h * GQA_HEAD_DIM, (h + 1) * GQA_HEAD_DIM)
        y = _head_rms(qb_ref[:, sl].astype(F32), qg)
        if rope:
            y = _rope(y, cb_ref[...], sb_ref[...], GQA_SEG)
        oqb_ref[sl, :] = (y * scale_b).T.astype(oqb_ref.dtype)
    cq = _head_rms(cq_ref[...].astype(F32), cqg_ref[...]).astype(BF16)
    qc = jnp.dot(cq, wuq_ref[...], preferred_element_type=F32)
    scale_c = LOG2_E * (MLA_NOPE + MLA_ROPE) ** -0.5
    for h in range(MLA_HEADS):
        nope = slice(h * MLA_QK_PAD, h * MLA_QK_PAD + MLA_NOPE)
        rot = slice(h * MLA_QK_PAD + MLA_NOPE, (h + 1) * MLA_QK_PAD)
        oqc_ref[nope, :] = (qc[:, nope] * scale_c).T.astype(oqc_ref.dtype)
        y = qc[:, rot]
        if rope:
            y = _rope(y, cc_ref[...], sc_ref[...], MLA_SEG)
        oqc_ref[rot, :] = (y * scale_c).T.astype(oqc_ref.dtype)


def qprep(p, off, qg, cqg, wuq, tables, tm=256):
    m = p.shape[0]
    tm = min(tm, m)
    wq = GQA_HEADS * GQA_HEAD_DIM
    qb_blk = (off + P_QB) // wq
    cq_blk = (off + P_CQ) // MLA_Q_RANK
    assert (off + P_QB) % wq == 0 and (off + P_CQ) % MLA_Q_RANK == 0
    in_specs = [pl.BlockSpec((tm, wq), lambda i: (i, qb_blk)),
                pl.BlockSpec((tm, MLA_Q_RANK), lambda i: (i, cq_blk)),
                pl.BlockSpec((1, GQA_HEAD_DIM), lambda i: (0, 0)),
                pl.BlockSpec((1, MLA_Q_RANK), lambda i: (0, 0)),
                pl.BlockSpec(wuq.shape, lambda i: (0, 0))]
    args = [p, p, qg.reshape(1, -1), cqg.reshape(1, -1), wuq]
    if tables is not None:
        in_specs += [pl.BlockSpec((tm, LANES), lambda i: (i, 0))] * 4
        args += list(tables)
    wc = MLA_HEADS * MLA_QK_PAD
    return pl.pallas_call(
        functools.partial(_qprep_kernel, rope=tables is not None),
        grid=(m // tm,),
        in_specs=in_specs,
        out_specs=[pl.BlockSpec((wq, tm), lambda i: (0, i)), pl.BlockSpec((wc, tm), lambda i: (0, i))],
        out_shape=[jax.ShapeDtypeStruct((wq, m), BF16), jax.ShapeDtypeStruct((wc, m), BF16)],
        compiler_params=_params("parallel"),
        name="qprep",
    )(*args)


def _kvprep_kernel(kb_ref, vb_ref, ckv_ref, kr_ref, kg_ref, ckvg_ref, wk_ref, wv_ref, *rest, rope):
    if rope:
        cb_ref, sb_ref, cc_ref, sc_ref, okb_ref, okc_ref, ovb_ref, ovc_ref = rest
    else:
        okb_ref, okc_ref, ovb_ref, ovc_ref = rest
    kg = kg_ref[...]
    for h in range(GQA_KV_HEADS):
        sl = slice(h * GQA_HEAD_DIM, (h + 1) * GQA_HEAD_DIM)
        y = _head_rms(kb_ref[:, sl].astype(F32), kg)
        if rope:
            y = _rope(y, cb_ref[...], sb_ref[...], GQA_SEG)
        okb_ref[:, sl] = y.astype(okb_ref.dtype)
    ovb_ref[...] = vb_ref[...].astype(F32).T.astype(ovb_ref.dtype)
    ckv = _head_rms(ckv_ref[...].astype(F32), ckvg_ref[...]).astype(BF16)
    knope = jnp.dot(ckv, wk_ref[...], preferred_element_type=F32)
    ovc_ref[...] = jnp.dot(ckv, wv_ref[...], preferred_element_type=F32).T.astype(ovc_ref.dtype)
    kr = kr_ref[...].astype(F32)
    if rope:
        kr = _rope(kr, cc_ref[...], sc_ref[...], MLA_SEG)
    kr = kr.astype(okc_ref.dtype)
    for h in range(MLA_HEADS):
        okc_ref[:, h * MLA_QK_PAD:h * MLA_QK_PAD + MLA_NOPE] = (
            knope[:, h * MLA_NOPE:(h + 1) * MLA_NOPE].astype(okc_ref.dtype))
        okc_ref[:, h * MLA_QK_PAD + MLA_NOPE:(h + 1) * MLA_QK_PAD] = kr


def kvprep(p, off, kg, ckvg, wk, wv, tables):
    m = p.shape[0]
    tm = min(ATTN_TK, m)
    kb_blk = (off + P_KB) // GQA_KV_W
    vb_blk = (off + P_VB) // GQA_KV_W
    ckv_blk = (off + P_CKV) // MLA_KV_RANK
    kr_blk = (off + P_KR) // LANES
    assert (off + P_KB) % GQA_KV_W == 0 and (off + P_CKV) % MLA_KV_RANK == 0 and (off + P_KR) % LANES == 0
    in_specs = [pl.BlockSpec((tm, GQA_KV_W), lambda i: (i, kb_blk)),
                pl.BlockSpec((tm, GQA_KV_W), lambda i: (i, vb_blk)),
                pl.BlockSpec((tm, MLA_KV_RANK), lambda i: (i, ckv_blk)),
                pl.BlockSpec((tm, LANES), lambda i: (i, kr_blk)),
                pl.BlockSpec((1, GQA_HEAD_DIM), lambda i: (0, 0)),
                pl.BlockSpec((1, MLA_KV_RANK), lambda i: (0, 0)),
                pl.BlockSpec(wk.shape, lambda i: (0, 0)),
                pl.BlockSpec(wv.shape, lambda i: (0, 0))]
    args = [p, p, p, p, kg.reshape(1, -1), ckvg.reshape(1, -1), wk, wv]
    if tables is not None:
        in_specs += [pl.BlockSpec((tm, LANES), lambda i: (i, 0))] * 4
        args += list(tables)
    wkc = MLA_HEADS * MLA_QK_PAD
    wvc = MLA_HEADS * MLA_V
    n = m // tm
    return pl.pallas_call(
        functools.partial(_kvprep_kernel, rope=tables is not None),
        grid=(n,),
        in_specs=in_specs,
        out_specs=[pl.BlockSpec((tm, GQA_KV_W), lambda i: (i, 0)),
                   pl.BlockSpec((tm, wkc), lambda i: (i, 0)),
                   pl.BlockSpec((None, GQA_KV_W, tm), lambda i: (i, 0, 0)),
                   pl.BlockSpec((None, wvc, tm), lambda i: (i, 0, 0))],
        out_shape=[jax.ShapeDtypeStruct((m, GQA_KV_W), BF16),
                   jax.ShapeDtypeStruct((m, wkc), BF16),
                   jax.ShapeDtypeStruct((n, GQA_KV_W, tm), BF16),
                   jax.ShapeDtypeStruct((n, wvc, tm), BF16)],
        compiler_params=_params("parallel"),
        name="kvprep",
    )(*args)


def _attn_kernel(*refs, has_x):
    if has_x:
        qt_ref, kc_ref, vct_ref, kx_ref, vxt_ref, o_ref, s_ref = refs
    else:
        qt_ref, kc_ref, vct_ref, o_ref = refs
    qt = qt_ref[...]
    st = jnp.dot(kc_ref[...], qt, preferred_element_type=F32)
    m = jnp.max(st, axis=0, keepdims=True)
    pt = jnp.exp2(st - m)
    l = jnp.sum(pt, axis=0, keepdims=True)
    acc = jnp.dot(vct_ref[...], pt.astype(BF16), preferred_element_type=F32)

    if has_x:
        n, _, tk = vxt_ref.shape

        def qk(j):
            st = jnp.dot(kx_ref[j * tk:(j + 1) * tk, :], qt, preferred_element_type=F32)
            s_ref[j % 2] = st
            return jnp.max(st, axis=0, keepdims=True)

        def pv(j, m_old, m_new, l, acc):
            pt = jnp.exp2(s_ref[j % 2] - m_new)
            alpha = jnp.exp2(m_old - m_new)
            l = alpha * l + jnp.sum(pt, axis=0, keepdims=True)
            acc = alpha * acc + jnp.dot(vxt_ref[j], pt.astype(BF16), preferred_element_type=F32)
            return l, acc

        m_old, m_new = m, jnp.maximum(m, qk(0))
        for j in range(n):
            m_next = jnp.maximum(m_new, qk(j + 1)) if j + 1 < n else m_new
            l, acc = pv(j, m_old, m_new, l, acc)
            m_old, m_new = m_new, m_next
    o_ref[...] = (acc / l).T.astype(o_ref.dtype)


def attention(qt, ctx_kv, x_kv, *, n_heads, group, dq, dv, tq=1024, name="attn"):
    s_len = qt.shape[1]
    tq = min(tq, s_len)
    kc, vct = ctx_kv
    tc = kc.shape[0]
    in_specs = [pl.BlockSpec((dq, tq), lambda h, i: (h, i)),
                pl.BlockSpec((tc, dq), lambda h, i: (0, h // group)),
                pl.BlockSpec((None, dv, tc), lambda h, i: (0, h // group, 0))]
    args = [qt, kc, vct]
    scratch = []
    if x_kv is not None:
        kx, vxt = x_kv
        in_specs += [pl.BlockSpec((kx.shape[0], dq), lambda h, i: (0, h // group)),
                     pl.BlockSpec((vxt.shape[0], dv, vxt.shape[2]), lambda h, i: (0, h // group, 0))]
        args += [kx, vxt]
        scratch = [pltpu.VMEM((2, vxt.shape[2], tq), F32)]
    return pl.pallas_call(
        functools.partial(_attn_kernel, has_x=x_kv is not None),
        grid=(n_heads, s_len // tq),
        in_specs=in_specs,
        out_specs=pl.BlockSpec((tq, dv), lambda h, i: (i, h)),
        out_shape=jax.ShapeDtypeStruct((s_len, n_heads * dv), BF16),
        scratch_shapes=scratch,
        compiler_params=_params("parallel", "arbitrary"),
        name=name,
    )(*args)


def _conv_kernel(gb_ref, gc_ref, xa_ref, gcp_ref, xap_ref, gcn_ref, xan_ref, w_ref, o_ref):
    i = pl.program_id(0)
    tm = gb_ref.shape[0]
    u = gc_ref[...].astype(F32) * xa_ref[...].astype(F32)
    halo = gcp_ref.shape[0]
    u_prev = gcp_ref[halo - 1:halo, :].astype(F32) * xap_ref[halo - 1:halo, :].astype(F32)
    u_next = gcn_ref[0:1, :].astype(F32) * xan_ref[0:1, :].astype(F32)
    u_prev = jnp.where(i == 0, 0.0, u_prev)
    u_next = jnp.where(i == pl.num_programs(0) - 1, 0.0, u_next)
    row = lax.broadcasted_iota(jnp.int32, u.shape, 0)
    below = jnp.where(row == 0, u_prev, pltpu.roll(u, 1, 0))
    above = jnp.where(row == tm - 1, u_next, pltpu.roll(u, tm - 1, 0))
    w = w_ref[...]
    y = below * w[0:1, :] + u * w[1:2, :] + above * w[2:3, :]
    o_ref[...] = (gb_ref[...].astype(F32) * y).astype(o_ref.dtype)


def short_conv(p, off, w, tm=512):
    m = p.shape[0]
    tm = min(tm, m)
    c = CONV_W
    blk = (off + P_A) // c
    assert (off + P_A) % c == 0
    hb = tm // BF16_SUBLANES
    n_hb = m // BF16_SUBLANES
    main = lambda k: pl.BlockSpec((tm, c), lambda i: (i, blk + k))
    prev = lambda k: pl.BlockSpec((BF16_SUBLANES, c), lambda i: (jnp.maximum(i * hb - 1, 0), blk + k))
    nxt = lambda k: pl.BlockSpec((BF16_SUBLANES, c), lambda i: (jnp.minimum((i + 1) * hb, n_hb - 1), blk + k))
    return pl.pallas_call(
        _conv_kernel,
        grid=(m // tm,),
        in_specs=[main(0), main(1), main(2), prev(1), prev(2), nxt(1), nxt(2),
                  pl.BlockSpec((CONV_K, c), lambda i: (0, 0))],
        out_specs=pl.BlockSpec((tm, c), lambda i: (i, 0)),
        out_shape=jax.ShapeDtypeStruct((m, c), BF16),
        compiler_params=_params("parallel"),
        name="short_conv",
    )(p, p, p, p, p, p, p, w)


def _gelu_tanh(x):
    c = math.sqrt(2.0 / math.pi)
    return x * (0.5 * (1.0 + jnp.tanh(c * (x + 0.044715 * (x * x * x)))))


def _sgu_kernel(u_ref, v_ref, g_ref, b_ref, ws_ref, bs_ref, o_ref):
    tm = u_ref.shape[0]
    n_chunk = tm // SGU_CHUNK
    gw = SGU_W // SGU_GROUPS
    v = _gelu_tanh(v_ref[...].astype(F32))
    mu = jnp.mean(v, axis=-1, keepdims=True)
    vc = v - mu
    vn = vc * lax.rsqrt(jnp.mean(vc * vc, axis=-1, keepdims=True) + EPS) * g_ref[...] + b_ref[...]
    vn = vn.astype(BF16)
    for g in range(SGU_GROUPS):
        cols = slice(g * gw, (g + 1) * gw)
        rhs = jnp.concatenate([vn[c * SGU_CHUNK:(c + 1) * SGU_CHUNK, cols] for c in range(n_chunk)], axis=1)
        mixed = jnp.dot(ws_ref[g], rhs, preferred_element_type=F32)
        for c in range(n_chunk):
            rows = slice(c * SGU_CHUNK, (c + 1) * SGU_CHUNK)
            u = _gelu_tanh(u_ref[rows, cols].astype(F32))
            o_ref[rows, cols] = (u * (mixed[:, c * gw:(c + 1) * gw] + bs_ref[g])).astype(o_ref.dtype)


def sgu(p, off, ln_g, ln_b, w_s, b_s, tm=512):
    m = p.shape[0]
    tm = min(tm, m)
    c = SGU_W
    blk = (off + P_D) // c
    assert (off + P_D) % c == 0
    gw = SGU_W // SGU_GROUPS
    bs = jnp.broadcast_to(b_s[:, :, None], (SGU_GROUPS, SGU_CHUNK, gw))
    return pl.pallas_call(
        _sgu_kernel,
        grid=(m // tm,),
        in_specs=[pl.BlockSpec((tm, c), lambda i: (i, blk)),
                  pl.BlockSpec((tm, c), lambda i: (i, blk + 1)),
                  pl.BlockSpec((1, c), lambda i: (0, 0)),
                  pl.BlockSpec((1, c), lambda i: (0, 0)),
                  pl.BlockSpec(w_s.shape, lambda i: (0, 0, 0)),
                  pl.BlockSpec(bs.shape, lambda i: (0, 0, 0))],
        out_specs=pl.BlockSpec((tm, c), lambda i: (i, 0)),
        out_shape=jax.ShapeDtypeStruct((m, c), BF16),
        compiler_params=_params("parallel"),
        name="sgu",
    )(p, p, ln_g.reshape(1, c), ln_b.reshape(1, c), w_s, bs)


def _merge_kernel(*refs):
    ys, pgs, bgs, wbs, o_ref = refs[0:4], refs[4:8], refs[8:12], refs[12:16], refs[16]
    acc = None
    for i in range(N_BRANCH):
        z = pgs[i][...].astype(F32) + bgs[i][...]
        gate = 1.0 / (1.0 + jnp.exp(-z))
        term = gate * jnp.dot(ys[i][...], wbs[i][...], preferred_element_type=F32)
        acc = term if acc is None else acc + term
    o_ref[...] = acc.astype(o_ref.dtype)


def merge(ys, p, off, b_gate, w_branch, l, tm=1024, tn=512):
    m = p.shape[0]
    tm = min(tm, m)
    nj = D_MODEL // tn
    g_blk = (off + P_G) // tn
    assert (off + P_G) % tn == 0
    in_specs = [pl.BlockSpec((tm, BRANCH_W), lambda i, j: (i, 0))] * N_BRANCH
    in_specs += [pl.BlockSpec((tm, tn), lambda i, j, b=b: (i, g_blk + b * nj + j)) for b in range(N_BRANCH)]
    in_specs += [pl.BlockSpec((1, tn), lambda i, j, b=b: (0, b * nj + j)) for b in range(N_BRANCH)]
    in_specs += [pl.BlockSpec((None, None, BRANCH_W, tn), lambda i, j, b=b: (l, b, 0, j))
                 for b in range(N_BRANCH)]
    return pl.pallas_call(
        _merge_kernel,
        grid=(m // tm, nj),
        in_specs=in_specs,
        out_specs=pl.BlockSpec((tm, tn), lambda i, j: (i, j)),
        out_shape=jax.ShapeDtypeStruct((m, D_MODEL), BF16),
        compiler_params=_params("parallel", "arbitrary"),
        name="merge",
    )(*ys, p, p, p, p, *([b_gate.reshape(1, -1)] * N_BRANCH), *([w_branch] * N_BRANCH))


def _rope_tables(n_tok):
    t = jnp.arange(n_tok)
    row = (t // GRID_W).astype(F32)[:, None]
    col = (t % GRID_W).astype(F32)[:, None]

    def table(rot_dim, pad):
        axis_dim = rot_dim // 2
        inv = ROPE_BASE ** (-jnp.arange(0, axis_dim, 2, dtype=F32) / axis_dim)
        ar, ac = row * inv, col * inv
        cos = jnp.concatenate([jnp.cos(ar), jnp.cos(ar), jnp.cos(ac), jnp.cos(ac)], axis=1)
        sin = jnp.concatenate([-jnp.sin(ar), jnp.sin(ar), -jnp.sin(ac), jnp.sin(ac)], axis=1)
        if pad:
            cos = jnp.pad(cos, ((0, 0), (0, pad)))
            sin = jnp.pad(sin, ((0, 0), (0, pad)))
        return cos, sin

    cb, sb = table(GQA_HEAD_DIM, 0)
    cc, sc = table(MLA_ROPE, LANES - MLA_ROPE)
    return cb, sb, cc, sc


def _prep_weights(w_in, w_uq, w_ukv, w_branch, w_ff2):
    n_l = w_in.shape[0]
    kv, q, a, d = KV_COLS, KV_COLS + Q_COLS, KV_COLS + Q_COLS + A_COLS, KV_COLS + Q_COLS + A_COLS + D_COLS
    pieces = [w_in[:, :, d:], w_in[:, :, q:a], w_in[:, :, a:d], w_in[:, :, kv:q], w_in[:, :, :kv]]
    pad = jnp.zeros((n_l, D_MODEL, P_COLS - (G_COLS + A_COLS + D_COLS + Q_COLS + KV_COLS)), BF16)
    w_in_p = jnp.concatenate([x.astype(BF16) for x in pieces] + [pad], axis=2)
    wuq = w_uq.reshape(n_l, MLA_Q_RANK, MLA_HEADS, MLA_NOPE + MLA_ROPE)
    wuq = jnp.pad(wuq, ((0, 0), (0, 0), (0, 0), (0, MLA_QK_PAD - MLA_NOPE - MLA_ROPE)))
    wuq = wuq.reshape(n_l, MLA_Q_RANK, MLA_HEADS * MLA_QK_PAD).astype(BF16)
    wukv = w_ukv.reshape(n_l, MLA_KV_RANK, MLA_HEADS, MLA_NOPE + MLA_V)
    wk = wukv[:, :, :, :MLA_NOPE].reshape(n_l, MLA_KV_RANK, MLA_HEADS * MLA_NOPE).astype(BF16)
    wv = wukv[:, :, :, MLA_NOPE:].reshape(n_l, MLA_KV_RANK, MLA_HEADS * MLA_V).astype(BF16)
    return dict(w_in=w_in_p, wuq=wuq, wk=wk, wv=wv, w_branch=w_branch.astype(BF16), w_ff2=w_ff2.astype(BF16))


def _kv_of(p, off, lw, lp, tables):
    return kvprep(p, off, lp["k_norm_g"], lp["mla_kv_norm_g"], lw["wk"], lw["wv"], tables)


def _mix(p, lw, lp, tables, ctx_kv, x_kv):
    qbt, qct = qprep(p, 0, lp["q_norm_g"], lp["mla_q_norm_g"], lw["wuq"], tables)
    gqa = lambda kv: None if kv is None else (kv[0], kv[2])
    mla = lambda kv: None if kv is None else (kv[1], kv[3])
    y_b = attention(qbt, gqa(ctx_kv), gqa(x_kv), n_heads=GQA_HEADS, group=GQA_GROUP, dq=GQA_HEAD_DIM,
                    dv=GQA_HEAD_DIM, name="attn_gqa")
    y_c = attention(qct, mla(ctx_kv), mla(x_kv), n_heads=MLA_HEADS, group=1, dq=MLA_QK_PAD, dv=MLA_V,
                    name="attn_mla")
    y_a = short_conv(p, 0, lp["conv_w"])
    y_d = sgu(p, 0, lp["sgu_ln_g"], lp["sgu_ln_b"], lw["sgu_w_s"], lp["sgu_b_s"])
    return merge([y_a, y_b, y_c, y_d], p, 0, lp["b_gate"], lw["w_branch"], lw["l"])


def kernel(x, c, ctx, c_ctx, w_ada, b_ada, norm_mix_g, w_in, b_gate, conv_w, q_norm_g, k_norm_g,
           mla_q_norm_g, mla_kv_norm_g, w_uq, w_ukv, sgu_ln_g, sgu_ln_b, sgu_w_s, sgu_b_s,
           w_branch, w_out, norm_ffn_g, w_ff1, w_ff2, final_norm_g):
    assert x.shape[0] == 1 and c.shape[0] == 1 and ctx.shape[0] == 1
    xs, zs = x[0], ctx[0]
    tables = _rope_tables(xs.shape[0])
    cond_t = jnp.stack([c[0], c_ctx], axis=1)
    X, Z = 0, 1

    wts = _prep_weights(w_in, w_uq, w_ukv, w_branch, w_ff2)
    b_ada3 = b_ada[:, None, :]

    def ffn(s, mod, row, l):
        h = norm_mod(s, norm_ffn_g[l], mod, row=row, k_shift=3, k_scale=4)
        u = matmul(h, w_ff1, l, epilogue="relu2", name="ffn_up")
        return matmul(u, wts["w_ff2"], l, out_dtype=F32, epilogue="residual", res=s, mod=mod, row=row,
                      k_gate=5, tm=512, tn=512, name="ffn_down")

    def out_proj(merged, s, mod, row, l):
        return matmul(merged, w_out, l, out_dtype=F32, epilogue="residual", res=s, mod=mod, row=row,
                      k_gate=2, tn=512, name="out_proj")

    for l in range(DEPTH):
        last = l == DEPTH - 1
        lw = dict(l=l, wuq=wts["wuq"][l], wk=wts["wk"][l], wv=wts["wv"][l], w_branch=wts["w_branch"],
                  sgu_w_s=sgu_w_s[l].astype(BF16))
        lp = dict(b_gate=b_gate[l], conv_w=conv_w[l], q_norm_g=q_norm_g[l], k_norm_g=k_norm_g[l],
                  mla_q_norm_g=mla_q_norm_g[l], mla_kv_norm_g=mla_kv_norm_g[l],
                  sgu_ln_g=sgu_ln_g[l], sgu_ln_b=sgu_ln_b[l], sgu_b_s=sgu_b_s[l])
        mod = adaln(cond_t, w_ada, b_ada3, l)

        hz = norm_mod(zs, norm_mix_g[l], mod, row=Z, k_shift=0, k_scale=1)
        if last:
            kv_w = P_COLS - P_KV_START
            pz = matmul(hz, wts["w_in"], l, n=kv_w, col0=P_KV_START // 512, tn=512, name="in_proj_ctx_kv")
            kv_z = _kv_of(pz, -P_KV_START, lw, lp, None)
        else:
            pz = matmul(hz, wts["w_in"], l, name="in_proj")
            kv_z = _kv_of(pz, 0, lw, lp, None)

        hx = norm_mod(xs, norm_mix_g[l], mod, row=X, k_shift=0, k_scale=1)
        px = matmul(hx, wts["w_in"], l, name="in_proj")
        kv_x = _kv_of(px, 0, lw, lp, tables)
        xs = out_proj(_mix(px, lw, lp, tables, kv_z, kv_x), xs, mod, X, l)
        xs = ffn(xs, mod, X, l)

        if not last:
            zs = out_proj(_mix(pz, lw, lp, None, kv_z, None), zs, mod, Z, l)
            zs = ffn(zs, mod, Z, l)

    out = norm_mod(xs, final_norm_g, out_dtype=F32)
    return out[None]
```

```python
import functools
import math

import jax
import jax.numpy as jnp
from jax import lax
from jax.experimental import pallas as pl
from jax.experimental.pallas import tpu as pltpu

F32 = jnp.float32
BF16 = jnp.bfloat16

D_MODEL = 2048
DEPTH = 2
GRID_W = 64
ROPE_BASE = 10000.0
EPS = 1e-6
N_BRANCH = 4
BRANCH_W = 1024
CONV_W = 1024
CONV_K = 3
GQA_HEADS = 8
GQA_KV_HEADS = 2
GQA_GROUP = GQA_HEADS // GQA_KV_HEADS
GQA_HEAD_DIM = 128
GQA_KV_W = GQA_KV_HEADS * GQA_HEAD_DIM
MLA_HEADS = 8
MLA_Q_RANK = 512
MLA_KV_RANK = 512
MLA_NOPE = 128
MLA_ROPE = 64
MLA_V = 128
SGU_W = 1024
SGU_GROUPS = 8
SGU_CHUNK = 128
D_FF = 4 * D_MODEL

KV_COLS = 2 * GQA_KV_W + MLA_KV_RANK + MLA_ROPE
Q_COLS = GQA_HEADS * GQA_HEAD_DIM + MLA_Q_RANK
A_COLS = 3 * CONV_W
D_COLS = 2 * SGU_W
G_COLS = N_BRANCH * D_MODEL

LANES = 128
BF16_SUBLANES = 16
MLA_QK_PAD = 256
ATTN_TK = 512
ATTN_AHEAD = 3
VMEM_LIMIT_BYTES = 48 * 1024 * 1024

P_G = 0
P_A = P_G + G_COLS
P_D = P_A + A_COLS
P_QB = P_D + D_COLS
P_CQ = P_QB + GQA_HEADS * GQA_HEAD_DIM
P_KB = P_CQ + MLA_Q_RANK
P_VB = P_KB + GQA_KV_W
P_CKV = P_VB + GQA_KV_W
P_KR = P_CKV + MLA_KV_RANK
P_USED = P_KR + LANES
P_COLS = 16384
P_KV_START = P_KB - (P_KB % 512)


def _params(*sem):
    return pltpu.CompilerParams(dimension_semantics=sem, vmem_limit_bytes=VMEM_LIMIT_BYTES)


def _adaln_kernel(ct_ref, w_ref, b_ref, o_ref):
    ct = ct_ref[...]
    s = ct / (1.0 + jnp.exp(-ct))
    w = w_ref[...]
    for r in range(2):
        o_ref[r:r + 1, :] = jnp.sum(w * s[:, r:r + 1], axis=0, keepdims=True) + b_ref[...]


def adaln(cond_t, w, b, l, tn=512):
    _, d, n = w.shape
    return pl.pallas_call(
        _adaln_kernel,
        grid=(n // tn,),
        in_specs=[pl.BlockSpec((d, 2), lambda j: (0, 0)),
                  pl.BlockSpec((None, d, tn), lambda j: (l, 0, j)),
                  pl.BlockSpec((None, 1, tn), lambda j: (l, 0, j))],
        out_specs=pl.BlockSpec((2, tn), lambda j: (0, j)),
        out_shape=jax.ShapeDtypeStruct((2, n), F32),
        compiler_params=_params("arbitrary"),
        name="adaln",
    )(cond_t, w, b)


def _norm_kernel(x_ref, g_ref, *rest, row):
    o_ref = rest[-1]
    x = x_ref[...]
    y = x * lax.rsqrt(jnp.mean(x * x, axis=-1, keepdims=True) + EPS) * g_ref[...]
    if len(rest) == 3:
        sh_ref, sc_ref = rest[0], rest[1]
        y = y * (1.0 + sc_ref[row:row + 1, :]) + sh_ref[row:row + 1, :]
    o_ref[...] = y.astype(o_ref.dtype)


def norm_mod(x, g, mod=None, row=0, k_shift=0, k_scale=1, out_dtype=BF16, tm=512):
    m, d = x.shape
    tm = min(tm, m)
    in_specs = [pl.BlockSpec((tm, d), lambda i: (i, 0)), pl.BlockSpec((1, d), lambda i: (0, 0))]
    args = [x, g.reshape(1, d)]
    if mod is not None:
        in_specs += [pl.BlockSpec((2, d), lambda i: (0, k_shift)), pl.BlockSpec((2, d), lambda i: (0, k_scale))]
        args += [mod, mod]
    return pl.pallas_call(
        functools.partial(_norm_kernel, row=row),
        grid=(m // tm,),
        in_specs=in_specs,
        out_specs=pl.BlockSpec((tm, d), lambda i: (i, 0)),
        out_shape=jax.ShapeDtypeStruct((m, d), out_dtype),
        compiler_params=_params("parallel"),
        name="norm_mod",
    )(*args)


def _mm_kernel(a_ref, w_ref, *rest, epilogue, row, cast_w):
    if cast_w:
        wb_ref = rest[-1]
        rest = rest[:-1]

        @pl.when(pl.program_id(1) == 0)
        def _():
            wb_ref[...] = w_ref[...].astype(BF16)

        w = wb_ref[...]
    else:
        w = w_ref[...]
    o_ref = rest[-1]
    acc = jnp.dot(a_ref[...], w, preferred_element_type=F32)
    if epilogue == "relu2":
        r = jnp.maximum(acc, 0.0)
        acc = r * r
    elif epilogue == "residual":
        res_ref, gate_ref = rest[0], rest[1]
        acc = res_ref[...] + gate_ref[row:row + 1, :] * acc
    o_ref[...] = acc.astype(o_ref.dtype)


def matmul(a, w, l, *, n=None, col0=0, out_dtype=BF16, epilogue="none", res=None, mod=None, row=0, k_gate=0,
           tm=1024, tn=1024, name="matmul"):
    m, k = a.shape
    n = w.shape[2] if n is None else n
    tm, tn = min(tm, m), min(tn, n)
    assert m % tm == 0 and n % tn == 0
    cast_w = w.dtype == F32
    order = (lambda f: (lambda j, i: f(i, j))) if cast_w else (lambda f: f)
    in_specs = [pl.BlockSpec((tm, k), order(lambda i, j: (i, 0))),
                pl.BlockSpec((None, k, tn), order(lambda i, j: (l, 0, col0 + j)))]
    args = [a, w]
    if epilogue == "residual":
        kb = k_gate * (n // tn)
        in_specs += [pl.BlockSpec((tm, tn), order(lambda i, j: (i, j))),
                     pl.BlockSpec((2, tn), order(lambda i, j: (0, kb + j)))]
        args += [res, mod]
    grid = (n // tn, m // tm) if cast_w else (m // tm, n // tn)
    return pl.pallas_call(
        functools.partial(_mm_kernel, epilogue=epilogue, row=row, cast_w=cast_w),
        grid=grid,
        in_specs=in_specs,
        out_specs=pl.BlockSpec((tm, tn), order(lambda i, j: (i, j))),
        out_shape=jax.ShapeDtypeStruct((m, n), out_dtype),
        scratch_shapes=[pltpu.VMEM((k, tn), BF16)] if cast_w else [],
        compiler_params=_params("parallel", "arbitrary"),
        name=name,
    )(*args)


def _rope(y, cos, sin, seg):
    lane = lax.broadcasted_iota(jnp.int32, y.shape, 1)
    first = (lane // seg) % 2 == 0
    partner = jnp.where(first, pltpu.roll(y, LANES - seg, 1), pltpu.roll(y, seg, 1))
    return y * cos + partner * sin


def _head_rms(x, g):
    return x * lax.rsqrt(jnp.mean(x * x, axis=-1, keepdims=True) + EPS) * g


LOG2_E = math.log2(math.e)
GQA_SEG = GQA_HEAD_DIM // 4
MLA_SEG = MLA_ROPE // 4


def _qprep_kernel(qb_ref, cq_ref, qg_ref, cqg_ref, wuq_ref, *rest, rope):
    if rope:
        cb_ref, sb_ref, cc_ref, sc_ref, oqb_ref, oqc_ref = rest
    else:
        oqb_ref, oqc_ref = rest
    qg = qg_ref[...]
    scale_b = LOG2_E * GQA_HEAD_DIM ** -0.5
    for h in range(GQA_HEADS):
        sl = slice(h * GQA_HEAD_DIM, (h + 1) * GQA_HEAD_DIM)
        y = _head_rms(qb_ref[:, sl].astype(F32), qg)
        if rope:
            y = _rope(y, cb_ref[...], sb_ref[...], GQA_SEG)
        oqb_ref[sl, :] = (y * scale_b).T.astype(oqb_ref.dtype)
    cq = _head_rms(cq_ref[...].astype(F32), cqg_ref[...]).astype(BF16)
    qc = jnp.dot(cq, wuq_ref[...], preferred_element_type=F32)
    scale_c = LOG2_E * (MLA_NOPE + MLA_ROPE) ** -0.5
    for h in range(MLA_HEADS):
        nope = slice(h * MLA_QK_PAD, h * MLA_QK_PAD + MLA_NOPE)
        rot = slice(h * MLA_QK_PAD + MLA_NOPE, (h + 1) * MLA_QK_PAD)
        oqc_ref[nope, :] = (qc[:, nope] * scale_c).T.astype(oqc_ref.dtype)
        y = qc[:, rot]
        if rope:
            y = _rope(y, cc_ref[...], sc_ref[...], MLA_SEG)
        oqc_ref[rot, :] = (y * scale_c).T.astype(oqc_ref.dtype)


def qprep(p, off, qg, cqg, wuq, tables, tm=256):
    m = p.shape[0]
    tm = min(tm, m)
    wq = GQA_HEADS * GQA_HEAD_DIM
    qb_blk = (off + P_QB) // wq
    cq_blk = (off + P_CQ) // MLA_Q_RANK
    assert (off + P_QB) % wq == 0 and (off + P_CQ) % MLA_Q_RANK == 0
    in_specs = [pl.BlockSpec((tm, wq), lambda i: (i, qb_blk)),
                pl.BlockSpec((tm, MLA_Q_RANK), lambda i: (i, cq_blk)),
                pl.BlockSpec((1, GQA_HEAD_DIM), lambda i: (0, 0)),
                pl.BlockSpec((1, MLA_Q_RANK), lambda i: (0, 0)),
                pl.BlockSpec(wuq.shape, lambda i: (0, 0))]
    args = [p, p, qg.reshape(1, -1), cqg.reshape(1, -1), wuq]
    if tables is not None:
        in_specs += [pl.BlockSpec((tm, LANES), lambda i: (i, 0))] * 4
        args += list(tables)
    wc = MLA_HEADS * MLA_QK_PAD
    return pl.pallas_call(
        functools.partial(_qprep_kernel, rope=tables is not None),
        grid=(m // tm,),
        in_specs=in_specs,
        out_specs=[pl.BlockSpec((wq, tm), lambda i: (0, i)), pl.BlockSpec((wc, tm), lambda i: (0, i))],
        out_shape=[jax.ShapeDtypeStruct((wq, m), BF16), jax.ShapeDtypeStruct((wc, m), BF16)],
        compiler_params=_params("parallel"),
        name="qprep",
    )(*args)


def _kvprep_kernel(kb_ref, vb_ref, ckv_ref, kr_ref, kg_ref, ckvg_ref, wk_ref, wv_ref, *rest, rope):
    if rope:
        cb_ref, sb_ref, cc_ref, sc_ref, okb_ref, okc_ref, ovb_ref, ovc_ref = rest
    else:
        okb_ref, okc_ref, ovb_ref, ovc_ref = rest
    kg = kg_ref[...]
    for h in range(GQA_KV_HEADS):
        sl = slice(h * GQA_HEAD_DIM, (h + 1) * GQA_HEAD_DIM)
        y = _head_rms(kb_ref[:, sl].astype(F32), kg)
        if rope:
            y = _rope(y, cb_ref[...], sb_ref[...], GQA_SEG)
        okb_ref[:, sl] = y.astype(okb_ref.dtype)
    ovb_ref[...] = vb_ref[...].astype(F32).T.astype(ovb_ref.dtype)
    ckv = _head_rms(ckv_ref[...].astype(F32), ckvg_ref[...]).astype(BF16)
    knope = jnp.dot(ckv, wk_ref[...], preferred_element_type=F32)
    ovc_ref[...] = jnp.dot(ckv, wv_ref[...], preferred_element_type=F32).T.astype(ovc_ref.dtype)
    kr = kr_ref[...].astype(F32)
    if rope:
        kr = _rope(kr, cc_ref[...], sc_ref[...], MLA_SEG)
    kr = kr.astype(okc_ref.dtype)
    for h in range(MLA_HEADS):
        okc_ref[:, h * MLA_QK_PAD:h * MLA_QK_PAD + MLA_NOPE] = (
            knope[:, h * MLA_NOPE:(h + 1) * MLA_NOPE].astype(okc_ref.dtype))
        okc_ref[:, h * MLA_QK_PAD + MLA_NOPE:(h + 1) * MLA_QK_PAD] = kr


def kvprep(p, off, kg, ckvg, wk, wv, tables):
    m = p.shape[0]
    tm = min(ATTN_TK, m)
    kb_blk = (off + P_KB) // GQA_KV_W
    vb_blk = (off + P_VB) // GQA_KV_W
    ckv_blk = (off + P_CKV) // MLA_KV_RANK
    kr_blk = (off + P_KR) // LANES
    assert (off + P_KB) % GQA_KV_W == 0 and (off + P_CKV) % MLA_KV_RANK == 0 and (off + P_KR) % LANES == 0
    in_specs = [pl.BlockSpec((tm, GQA_KV_W), lambda i: (i, kb_blk)),
                pl.BlockSpec((tm, GQA_KV_W), lambda i: (i, vb_blk)),
                pl.BlockSpec((tm, MLA_KV_RANK), lambda i: (i, ckv_blk)),
                pl.BlockSpec((tm, LANES), lambda i: (i, kr_blk)),
                pl.BlockSpec((1, GQA_HEAD_DIM), lambda i: (0, 0)),
                pl.BlockSpec((1, MLA_KV_RANK), lambda i: (0, 0)),
                pl.BlockSpec(wk.shape, lambda i: (0, 0)),
                pl.BlockSpec(wv.shape, lambda i: (0, 0))]
    args = [p, p, p, p, kg.reshape(1, -1), ckvg.reshape(1, -1), wk, wv]
    if tables is not None:
        in_specs += [pl.BlockSpec((tm, LANES), lambda i: (i, 0))] * 4
        args += list(tables)
    wkc = MLA_HEADS * MLA_QK_PAD
    wvc = MLA_HEADS * MLA_V
    n = m // tm
    return pl.pallas_call(
        functools.partial(_kvprep_kernel, rope=tables is not None),
        grid=(n,),
        in_specs=in_specs,
        out_specs=[pl.BlockSpec((tm, GQA_KV_W), lambda i: (i, 0)),
                   pl.BlockSpec((tm, wkc), lambda i: (i, 0)),
                   pl.BlockSpec((None, GQA_KV_W, tm), lambda i: (i, 0, 0)),
                   pl.BlockSpec((None, wvc, tm), lambda i: (i, 0, 0))],
        out_shape=[jax.ShapeDtypeStruct((m, GQA_KV_W), BF16),
                   jax.ShapeDtypeStruct((m, wkc), BF16),
                   jax.ShapeDtypeStruct((n, GQA_KV_W, tm), BF16),
                   jax.ShapeDtypeStruct((n, wvc, tm), BF16)],
        compiler_params=_params("parallel"),
        name="kvprep",
    )(*args)


def _attn_kernel(*refs, has_x):
    if has_x:
        qt_ref, kc_ref, vct_ref, kx_ref, vxt_ref, o_ref, s_ref = refs
        n, _, tk = vxt_ref.shape
    else:
        qt_ref, kc_ref, vct_ref, o_ref, s_ref = refs
        n, tk = 0, 0
    qt = qt_ref[...]
    n_slot = s_ref.shape[0]
    rows = [kc_ref.shape[0]] + [tk] * n

    def keys(g):
        return kc_ref[...] if g == 0 else kx_ref[(g - 1) * tk:g * tk, :]

    def values_t(g):
        return vct_ref[...] if g == 0 else vxt_ref[g - 1]

    def qk(g):
        st = jnp.dot(keys(g), qt, preferred_element_type=F32)
        s_ref[g % n_slot, :rows[g], :] = st
        return jnp.max(st, axis=0, keepdims=True)

    chunk_max = {g: qk(g) for g in range(min(ATTN_AHEAD, n + 1))}
    m = l = acc = None
    for g in range(n + 1):
        if g + ATTN_AHEAD <= n:
            chunk_max[g + ATTN_AHEAD] = qk(g + ATTN_AHEAD)
        m_new = chunk_max.pop(g) if m is None else jnp.maximum(m, chunk_max.pop(g))
        pt = jnp.exp2(s_ref[g % n_slot, :rows[g], :] - m_new)
        p_sum = jnp.sum(pt, axis=0, keepdims=True)
        pv = jnp.dot(values_t(g), pt.astype(BF16), preferred_element_type=F32)
        if m is None:
            l, acc = p_sum, pv
        else:
            alpha = jnp.exp2(m - m_new)
            l, acc = alpha * l + p_sum, alpha * acc + pv
        m = m_new
    o_ref[...] = (acc / l).T.astype(o_ref.dtype)


def attention(qt, ctx_kv, x_kv, *, n_heads, group, dq, dv, tq=512, name="attn"):
    s_len = qt.shape[1]
    tq = min(tq, s_len)
    kc, vct = ctx_kv
    tc = kc.shape[0]
    in_specs = [pl.BlockSpec((dq, tq), lambda h, i: (h, i)),
                pl.BlockSpec((tc, dq), lambda h, i: (0, h // group)),
                pl.BlockSpec((None, dv, tc), lambda h, i: (0, h // group, 0))]
    args = [qt, kc, vct]
    n_chunk, rows = 1, tc
    if x_kv is not None:
        kx, vxt = x_kv
        in_specs += [pl.BlockSpec((kx.shape[0], dq), lambda h, i: (0, h // group)),
                     pl.BlockSpec((vxt.shape[0], dv, vxt.shape[2]), lambda h, i: (0, h // group, 0))]
        args += [kx, vxt]
        n_chunk, rows = 1 + vxt.shape[0], max(tc, vxt.shape[2])
    return pl.pallas_call(
        functools.partial(_attn_kernel, has_x=x_kv is not None),
        grid=(n_heads, s_len // tq),
        in_specs=in_specs,
        out_specs=pl.BlockSpec((tq, dv), lambda h, i: (i, h)),
        out_shape=jax.ShapeDtypeStruct((s_len, n_heads * dv), BF16),
        scratch_shapes=[pltpu.VMEM((min(ATTN_AHEAD + 1, n_chunk), rows, tq), F32)],
        compiler_params=_params("parallel", "arbitrary"),
        name=name,
    )(*args)


def _conv_kernel(gb_ref, gc_ref, xa_ref, gcp_ref, xap_ref, gcn_ref, xan_ref, w_ref, o_ref):
    i = pl.program_id(0)
    tm = gb_ref.shape[0]
    u = gc_ref[...].astype(F32) * xa_ref[...].astype(F32)
    halo = gcp_ref.shape[0]
    u_prev = gcp_ref[halo - 1:halo, :].astype(F32) * xap_ref[halo - 1:halo, :].astype(F32)
    u_next = gcn_ref[0:1, :].astype(F32) * xan_ref[0:1, :].astype(F32)
    u_prev = jnp.where(i == 0, 0.0, u_prev)
    u_next = jnp.where(i == pl.num_programs(0) - 1, 0.0, u_next)
    row = lax.broadcasted_iota(jnp.int32, u.shape, 0)
    below = jnp.where(row == 0, u_prev, pltpu.roll(u, 1, 0))
    above = jnp.where(row == tm - 1, u_next, pltpu.roll(u, tm - 1, 0))
    w = w_ref[...]
    y = below * w[0:1, :] + u * w[1:2, :] + above * w[2:3, :]
    o_ref[...] = (gb_ref[...].astype(F32) * y).astype(o_ref.dtype)


def short_conv(p, off, w, tm=512):
    m = p.shape[0]
    tm = min(tm, m)
    c = CONV_W
    blk = (off + P_A) // c
    assert (off + P_A) % c == 0
    hb = tm // BF16_SUBLANES
    n_hb = m // BF16_SUBLANES
    main = lambda k: pl.BlockSpec((tm, c), lambda i: (i, blk + k))
    prev = lambda k: pl.BlockSpec((BF16_SUBLANES, c), lambda i: (jnp.maximum(i * hb - 1, 0), blk + k))
    nxt = lambda k: pl.BlockSpec((BF16_SUBLANES, c), lambda i: (jnp.minimum((i + 1) * hb, n_hb - 1), blk + k))
    return pl.pallas_call(
        _conv_kernel,
        grid=(m // tm,),
        in_specs=[main(0), main(1), main(2), prev(1), prev(2), nxt(1), nxt(2),
                  pl.BlockSpec((CONV_K, c), lambda i: (0, 0))],
        out_specs=pl.BlockSpec((tm, c), lambda i: (i, 0)),
        out_shape=jax.ShapeDtypeStruct((m, c), BF16),
        compiler_params=_params("parallel"),
        name="short_conv",
    )(p, p, p, p, p, p, p, w)


def _gelu_tanh(x):
    c = math.sqrt(2.0 / math.pi)
    return x * (0.5 * (1.0 + jnp.tanh(c * (x + 0.044715 * (x * x * x)))))


def _sgu_kernel(u_ref, v_ref, g_ref, b_ref, ws_ref, bs_ref, o_ref):
    tm = u_ref.shape[0]
    n_chunk = tm // SGU_CHUNK
    gw = SGU_W // SGU_GROUPS
    v = _gelu_tanh(v_ref[...].astype(F32))
    mu = jnp.mean(v, axis=-1, keepdims=True)
    vc = v - mu
    vn = vc * lax.rsqrt(jnp.mean(vc * vc, axis=-1, keepdims=True) + EPS) * g_ref[...] + b_ref[...]
    vn = vn.astype(BF16)
    for g in range(SGU_GROUPS):
        cols = slice(g * gw, (g + 1) * gw)
        rhs = jnp.concatenate([vn[c * SGU_CHUNK:(c + 1) * SGU_CHUNK, cols] for c in range(n_chunk)], axis=1)
        mixed = jnp.dot(ws_ref[g], rhs, preferred_element_type=F32)
        for c in range(n_chunk):
            rows = slice(c * SGU_CHUNK, (c + 1) * SGU_CHUNK)
            u = _gelu_tanh(u_ref[rows, cols].astype(F32))
            o_ref[rows, cols] = (u * (mixed[:, c * gw:(c + 1) * gw] + bs_ref[g])).astype(o_ref.dtype)


def sgu(p, off, ln_g, ln_b, w_s, b_s, tm=512):
    m = p.shape[0]
    tm = min(tm, m)
    c = SGU_W
    blk = (off + P_D) // c
    assert (off + P_D) % c == 0
    gw = SGU_W // SGU_GROUPS
    bs = jnp.broadcast_to(b_s[:, :, None], (SGU_GROUPS, SGU_CHUNK, gw))
    return pl.pallas_call(
        _sgu_kernel,
        grid=(m // tm,),
        in_specs=[pl.BlockSpec((tm, c), lambda i: (i, blk)),
                  pl.BlockSpec((tm, c), lambda i: (i, blk + 1)),
                  pl.BlockSpec((1, c), lambda i: (0, 0)),
                  pl.BlockSpec((1, c), lambda i: (0, 0)),
                  pl.BlockSpec(w_s.shape, lambda i: (0, 0, 0)),
                  pl.BlockSpec(bs.shape, lambda i: (0, 0, 0))],
        out_specs=pl.BlockSpec((tm, c), lambda i: (i, 0)),
        out_shape=jax.ShapeDtypeStruct((m, c), BF16),
        compiler_params=_params("parallel"),
        name="sgu",
    )(p, p, ln_g.reshape(1, c), ln_b.reshape(1, c), w_s, bs)


def _merge_kernel(*refs):
    ys, pgs, bgs, wbs, o_ref = refs[0:4], refs[4:8], refs[8:12], refs[12:16], refs[16]
    acc = None
    for i in range(N_BRANCH):
        z = pgs[i][...].astype(F32) + bgs[i][...]
        gate = 1.0 / (1.0 + jnp.exp(-z))
        term = gate * jnp.dot(ys[i][...], wbs[i][...], preferred_element_type=F32)
        acc = term if acc is None else acc + term
    o_ref[...] = acc.astype(o_ref.dtype)


def merge(ys, p, off, b_gate, w_branch, l, tm=1024, tn=512):
    m = p.shape[0]
    tm = min(tm, m)
    nj = D_MODEL // tn
    g_blk = (off + P_G) // tn
    assert (off + P_G) % tn == 0
    in_specs = [pl.BlockSpec((tm, BRANCH_W), lambda i, j: (i, 0))] * N_BRANCH
    in_specs += [pl.BlockSpec((tm, tn), lambda i, j, b=b: (i, g_blk + b * nj + j)) for b in range(N_BRANCH)]
    in_specs += [pl.BlockSpec((1, tn), lambda i, j, b=b: (0, b * nj + j)) for b in range(N_BRANCH)]
    in_specs += [pl.BlockSpec((None, None, BRANCH_W, tn), lambda i, j, b=b: (l, b, 0, j))
                 for b in range(N_BRANCH)]
    return pl.pallas_call(
        _merge_kernel,
        grid=(m // tm, nj),
        in_specs=in_specs,
        out_specs=pl.BlockSpec((tm, tn), lambda i, j: (i, j)),
        out_shape=jax.ShapeDtypeStruct((m, D_MODEL), BF16),
        compiler_params=_params("parallel", "arbitrary"),
        name="merge",
    )(*ys, p, p, p, p, *([b_gate.reshape(1, -1)] * N_BRANCH), *([w_branch] * N_BRANCH))


def _rope_tables(n_tok):
    t = jnp.arange(n_tok)
    row = (t // GRID_W).astype(F32)[:, None]
    col = (t % GRID_W).astype(F32)[:, None]

    def table(rot_dim, pad):
        axis_dim = rot_dim // 2
        inv = ROPE_BASE ** (-jnp.arange(0, axis_dim, 2, dtype=F32) / axis_dim)
        ar, ac = row * inv, col * inv
        cos = jnp.concatenate([jnp.cos(ar), jnp.cos(ar), jnp.cos(ac), jnp.cos(ac)], axis=1)
        sin = jnp.concatenate([-jnp.sin(ar), jnp.sin(ar), -jnp.sin(ac), jnp.sin(ac)], axis=1)
        if pad:
            cos = jnp.pad(cos, ((0, 0), (0, pad)))
            sin = jnp.pad(sin, ((0, 0), (0, pad)))
        return cos, sin

    cb, sb = table(GQA_HEAD_DIM, 0)
    cc, sc = table(MLA_ROPE, LANES - MLA_ROPE)
    return cb, sb, cc, sc


def _w_in_relayout_kernel(w_ref, o_ref):
    kv, q, a, d = KV_COLS, KV_COLS + Q_COLS, KV_COLS + Q_COLS + A_COLS, KV_COLS + Q_COLS + A_COLS + D_COLS
    for dst, lo, hi in ((P_G, d, d + G_COLS), (P_A, q, a), (P_D, a, d), (P_QB, kv, q), (P_KB, 0, kv)):
        o_ref[:, dst:dst + hi - lo] = w_ref[:, lo:hi].astype(o_ref.dtype)
    used = P_KB + KV_COLS
    o_ref[:, used:] = jnp.zeros((o_ref.shape[0], P_COLS - used), o_ref.dtype)


def w_in_relayout(w_in, tr=128):
    n_l, d, n = w_in.shape
    return pl.pallas_call(
        _w_in_relayout_kernel,
        grid=(n_l, d // tr),
        in_specs=[pl.BlockSpec((None, tr, n), lambda l, i: (l, i, 0))],
        out_specs=pl.BlockSpec((None, tr, P_COLS), lambda l, i: (l, i, 0)),
        out_shape=jax.ShapeDtypeStruct((n_l, d, P_COLS), BF16),
        compiler_params=_params("parallel", "parallel"),
        name="w_in_relayout",
    )(w_in)


def _prep_weights(w_in, w_uq, w_ukv, w_branch, w_ff2):
    n_l = w_in.shape[0]
    w_in_p = w_in_relayout(w_in)
    wuq = w_uq.reshape(n_l, MLA_Q_RANK, MLA_HEADS, MLA_NOPE + MLA_ROPE)
    wuq = jnp.pad(wuq, ((0, 0), (0, 0), (0, 0), (0, MLA_QK_PAD - MLA_NOPE - MLA_ROPE)))
    wuq = wuq.reshape(n_l, MLA_Q_RANK, MLA_HEADS * MLA_QK_PAD).astype(BF16)
    wukv = w_ukv.reshape(n_l, MLA_KV_RANK, MLA_HEADS, MLA_NOPE + MLA_V)
    wk = wukv[:, :, :, :MLA_NOPE].reshape(n_l, MLA_KV_RANK, MLA_HEADS * MLA_NOPE).astype(BF16)
    wv = wukv[:, :, :, MLA_NOPE:].reshape(n_l, MLA_KV_RANK, MLA_HEADS * MLA_V).astype(BF16)
    return dict(w_in=w_in_p, wuq=wuq, wk=wk, wv=wv, w_branch=w_branch.astype(BF16), w_ff2=w_ff2.astype(BF16))


def _kv_of(p, off, lw, lp, tables):
    return kvprep(p, off, lp["k_norm_g"], lp["mla_kv_norm_g"], lw["wk"], lw["wv"], tables)


def _mix(p, lw, lp, tables, ctx_kv, x_kv):
    qbt, qct = qprep(p, 0, lp["q_norm_g"], lp["mla_q_norm_g"], lw["wuq"], tables)
    gqa = lambda kv: None if kv is None else (kv[0], kv[2])
    mla = lambda kv: None if kv is None else (kv[1], kv[3])
    y_b = attention(qbt, gqa(ctx_kv), gqa(x_kv), n_heads=GQA_HEADS, group=GQA_GROUP, dq=GQA_HEAD_DIM,
                    dv=GQA_HEAD_DIM, name="attn_gqa")
    y_c = attention(qct, mla(ctx_kv), mla(x_kv), n_heads=MLA_HEADS, group=1, dq=MLA_QK_PAD, dv=MLA_V,
                    name="attn_mla")
    y_a = short_conv(p, 0, lp["conv_w"])
    y_d = sgu(p, 0, lp["sgu_ln_g"], lp["sgu_ln_b"], lw["sgu_w_s"], lp["sgu_b_s"])
    return merge([y_a, y_b, y_c, y_d], p, 0, lp["b_gate"], lw["w_branch"], lw["l"])


def kernel(x, c, ctx, c_ctx, w_ada, b_ada, norm_mix_g, w_in, b_gate, conv_w, q_norm_g, k_norm_g,
           mla_q_norm_g, mla_kv_norm_g, w_uq, w_ukv, sgu_ln_g, sgu_ln_b, sgu_w_s, sgu_b_s,
           w_branch, w_out, norm_ffn_g, w_ff1, w_ff2, final_norm_g):
    assert x.shape[0] == 1 and c.shape[0] == 1 and ctx.shape[0] == 1
    xs, zs = x[0], ctx[0]
    tables = _rope_tables(xs.shape[0])
    cond_t = jnp.stack([c[0], c_ctx], axis=1)
    X, Z = 0, 1

    wts = _prep_weights(w_in, w_uq, w_ukv, w_branch, w_ff2)
    b_ada3 = b_ada[:, None, :]

    def ffn(s, mod, row, l):
        h = norm_mod(s, norm_ffn_g[l], mod, row=row, k_shift=3, k_scale=4)
        u = matmul(h, w_ff1, l, epilogue="relu2", name="ffn_up")
        return matmul(u, wts["w_ff2"], l, out_dtype=F32, epilogue="residual", res=s, mod=mod, row=row,
                      k_gate=5, tm=512, tn=512, name="ffn_down")

    def out_proj(merged, s, mod, row, l):
        return matmul(merged, w_out, l, out_dtype=F32, epilogue="residual", res=s, mod=mod, row=row,
                      k_gate=2, tn=512, name="out_proj")

    for l in range(DEPTH):
        last = l == DEPTH - 1
        lw = dict(l=l, wuq=wts["wuq"][l], wk=wts["wk"][l], wv=wts["wv"][l], w_branch=wts["w_branch"],
                  sgu_w_s=sgu_w_s[l].astype(BF16))
        lp = dict(b_gate=b_gate[l], conv_w=conv_w[l], q_norm_g=q_norm_g[l], k_norm_g=k_norm_g[l],
                  mla_q_norm_g=mla_q_norm_g[l], mla_kv_norm_g=mla_kv_norm_g[l],
                  sgu_ln_g=sgu_ln_g[l], sgu_ln_b=sgu_ln_b[l], sgu_b_s=sgu_b_s[l])
        mod = adaln(cond_t, w_ada, b_ada3, l)

        hz = norm_mod(zs, norm_mix_g[l], mod, row=Z, k_shift=0, k_scale=1)
        if last:
            kv_w = P_COLS - P_KV_START
            pz = matmul(hz, wts["w_in"], l, n=kv_w, col0=P_KV_START // 512, tn=512, name="in_proj_ctx_kv")
            kv_z = _kv_of(pz, -P_KV_START, lw, lp, None)
        else:
            pz = matmul(hz, wts["w_in"], l, name="in_proj")
            kv_z = _kv_of(pz, 0, lw, lp, None)

        hx = norm_mod(xs, norm_mix_g[l], mod, row=X, k_shift=0, k_scale=1)
        px = matmul(hx, wts["w_in"], l, name="in_proj")
        kv_x = _kv_of(px, 0, lw, lp, tables)
        xs = out_proj(_mix(px, lw, lp, tables, kv_z, kv_x), xs, mod, X, l)
        xs = ffn(xs, mod, X, l)

        if not last:
            zs = out_proj(_mix(pz, lw, lp, None, kv_z, None), zs, mod, Z, l)
            zs = ffn(zs, mod, Z, l)

    out = norm_mod(xs, final_norm_g, out_dtype=F32)
    return out[None]
```

```python
import functools
import math

import jax
import jax.numpy as jnp
from jax import lax
from jax.experimental import pallas as pl
from jax.experimental.pallas import tpu as pltpu

F32 = jnp.float32
BF16 = jnp.bfloat16

D_MODEL = 2048
DEPTH = 2
GRID_W = 64
ROPE_BASE = 10000.0
EPS = 1e-6
N_BRANCH = 4
BRANCH_W = 1024
CONV_W = 1024
CONV_K = 3
GQA_HEADS = 8
GQA_KV_HEADS = 2
GQA_GROUP = GQA_HEADS // GQA_KV_HEADS
GQA_HEAD_DIM = 128
GQA_KV_W = GQA_KV_HEADS * GQA_HEAD_DIM
MLA_HEADS = 8
MLA_Q_RANK = 512
MLA_KV_RANK = 512
MLA_NOPE = 128
MLA_ROPE = 64
MLA_V = 128
SGU_W = 1024
SGU_GROUPS = 8
SGU_CHUNK = 128
D_FF = 4 * D_MODEL

KV_COLS = 2 * GQA_KV_W + MLA_KV_RANK + MLA_ROPE
Q_COLS = GQA_HEADS * GQA_HEAD_DIM + MLA_Q_RANK
A_COLS = 3 * CONV_W
D_COLS = 2 * SGU_W
G_COLS = N_BRANCH * D_MODEL

LANES = 128
BF16_SUBLANES = 16
MLA_QK_PAD = 256
ATTN_TK = 512
ATTN_R = 512
ATTN_BLOCKS = 2
ATTN_AHEAD = 3
VMEM_LIMIT_BYTES = 48 * 1024 * 1024

P_G = 0
P_A = P_G + G_COLS
P_D = P_A + A_COLS
P_QB = P_D + D_COLS
P_CQ = P_QB + GQA_HEADS * GQA_HEAD_DIM
P_KB = P_CQ + MLA_Q_RANK
P_VB = P_KB + GQA_KV_W
P_CKV = P_VB + GQA_KV_W
P_KR = P_CKV + MLA_KV_RANK
P_USED = P_KR + LANES
P_COLS = 16384
P_KV_START = P_KB - (P_KB % 512)


def _params(*sem):
    return pltpu.CompilerParams(dimension_semantics=sem, vmem_limit_bytes=VMEM_LIMIT_BYTES)


def _adaln_kernel(ct_ref, w_ref, b_ref, o_ref):
    ct = ct_ref[...]
    s = ct / (1.0 + jnp.exp(-ct))
    w = w_ref[...]
    for r in range(2):
        o_ref[r:r + 1, :] = jnp.sum(w * s[:, r:r + 1], axis=0, keepdims=True) + b_ref[...]


def adaln(cond_t, w, b, l, tn=512):
    _, d, n = w.shape
    return pl.pallas_call(
        _adaln_kernel,
        grid=(n // tn,),
        in_specs=[pl.BlockSpec((d, 2), lambda j: (0, 0)),
                  pl.BlockSpec((None, d, tn), lambda j: (l, 0, j)),
                  pl.BlockSpec((None, 1, tn), lambda j: (l, 0, j))],
        out_specs=pl.BlockSpec((2, tn), lambda j: (0, j)),
        out_shape=jax.ShapeDtypeStruct((2, n), F32),
        compiler_params=_params("arbitrary"),
        name="adaln",
    )(cond_t, w, b)


def _norm_kernel(x_ref, g_ref, *rest, row):
    o_ref = rest[-1]
    x = x_ref[...]
    y = x * lax.rsqrt(jnp.mean(x * x, axis=-1, keepdims=True) + EPS) * g_ref[...]
    if len(rest) == 3:
        sh_ref, sc_ref = rest[0], rest[1]
        y = y * (1.0 + sc_ref[row:row + 1, :]) + sh_ref[row:row + 1, :]
    o_ref[...] = y.astype(o_ref.dtype)


def norm_mod(x, g, mod=None, row=0, k_shift=0, k_scale=1, out_dtype=BF16, tm=512):
    m, d = x.shape
    tm = min(tm, m)
    in_specs = [pl.BlockSpec((tm, d), lambda i: (i, 0)), pl.BlockSpec((1, d), lambda i: (0, 0))]
    args = [x, g.reshape(1, d)]
    if mod is not None:
        in_specs += [pl.BlockSpec((2, d), lambda i: (0, k_shift)), pl.BlockSpec((2, d), lambda i: (0, k_scale))]
        args += [mod, mod]
    return pl.pallas_call(
        functools.partial(_norm_kernel, row=row),
        grid=(m // tm,),
        in_specs=in_specs,
        out_specs=pl.BlockSpec((tm, d), lambda i: (i, 0)),
        out_shape=jax.ShapeDtypeStruct((m, d), out_dtype),
        compiler_params=_params("parallel"),
        name="norm_mod",
    )(*args)


def _mm_kernel(a_ref, w_ref, *rest, epilogue, row, cast_w):
    if cast_w:
        wb_ref = rest[-1]
        rest = rest[:-1]

        @pl.when(pl.program_id(1) == 0)
        def _():
            wb_ref[...] = w_ref[...].astype(BF16)

        w = wb_ref[...]
    else:
        w = w_ref[...]
    o_ref = rest[-1]
    acc = jnp.dot(a_ref[...], w, preferred_element_type=F32)
    if epilogue == "relu2":
        r = jnp.maximum(acc, 0.0)
        acc = r * r
    elif epilogue == "residual":
        res_ref, gate_ref = rest[0], rest[1]
        acc = res_ref[...] + gate_ref[row:row + 1, :] * acc
    o_ref[...] = acc.astype(o_ref.dtype)


def matmul(a, w, l, *, n=None, col0=0, out_dtype=BF16, epilogue="none", res=None, mod=None, row=0, k_gate=0,
           tm=1024, tn=1024, name="matmul"):
    m, k = a.shape
    n = w.shape[2] if n is None else n
    tm, tn = min(tm, m), min(tn, n)
    assert m % tm == 0 and n % tn == 0
    cast_w = w.dtype == F32
    order = (lambda f: (lambda j, i: f(i, j))) if cast_w else (lambda f: f)
    in_specs = [pl.BlockSpec((tm, k), order(lambda i, j: (i, 0))),
                pl.BlockSpec((None, k, tn), order(lambda i, j: (l, 0, col0 + j)))]
    args = [a, w]
    if epilogue == "residual":
        kb = k_gate * (n // tn)
        in_specs += [pl.BlockSpec((tm, tn), order(lambda i, j: (i, j))),
                     pl.BlockSpec((2, tn), order(lambda i, j: (0, kb + j)))]
        args += [res, mod]
    grid = (n // tn, m // tm) if cast_w else (m // tm, n // tn)
    return pl.pallas_call(
        functools.partial(_mm_kernel, epilogue=epilogue, row=row, cast_w=cast_w),
        grid=grid,
        in_specs=in_specs,
        out_specs=pl.BlockSpec((tm, tn), order(lambda i, j: (i, j))),
        out_shape=jax.ShapeDtypeStruct((m, n), out_dtype),
        scratch_shapes=[pltpu.VMEM((k, tn), BF16)] if cast_w else [],
        compiler_params=_params("parallel", "arbitrary"),
        name=name,
    )(*args)


def _rope(y, cos, sin, seg):
    lane = lax.broadcasted_iota(jnp.int32, y.shape, 1)
    first = (lane // seg) % 2 == 0
    partner = jnp.where(first, pltpu.roll(y, LANES - seg, 1), pltpu.roll(y, seg, 1))
    return y * cos + partner * sin


def _head_rms(x, g):
    return x * lax.rsqrt(jnp.mean(x * x, axis=-1, keepdims=True) + EPS) * g


LOG2_E = math.log2(math.e)
GQA_SEG = GQA_HEAD_DIM // 4
MLA_SEG = MLA_ROPE // 4


def _qprep_kernel(qb_ref, cq_ref, qg_ref, cqg_ref, wuq_ref, *rest, rope):
    if rope:
        cb_ref, sb_ref, cc_ref, sc_ref, oqb_ref, oqc_ref = rest
    else:
        oqb_ref, oqc_ref = rest
    qg = qg_ref[...]
    scale_b = LOG2_E * GQA_HEAD_DIM ** -0.5
    for h in range(GQA_HEADS):
        sl = slice(h * GQA_HEAD_DIM, (h + 1) * GQA_HEAD_DIM)
        y = _head_rms(qb_ref[:, sl].astype(F32), qg)
        if rope:
            y = _rope(y, cb_ref[...], sb_ref[...], GQA_SEG)
        oqb_ref[sl, :] = (y * scale_b).T.astype(oqb_ref.dtype)
    cq = _head_rms(cq_ref[...].astype(F32), cqg_ref[...]).astype(BF16)
    qc = jnp.dot(cq, wuq_ref[...], preferred_element_type=F32)
    scale_c = LOG2_E * (MLA_NOPE + MLA_ROPE) ** -0.5
    for h in range(MLA_HEADS):
        nope = slice(h * MLA_QK_PAD, h * MLA_QK_PAD + MLA_NOPE)
        rot = slice(h * MLA_QK_PAD + MLA_NOPE, (h + 1) * MLA_QK_PAD)
        oqc_ref[nope, :] = (qc[:, nope] * scale_c).T.astype(oqc_ref.dtype)
        y = qc[:, rot]
        if rope:
            y = _rope(y, cc_ref[...], sc_ref[...], MLA_SEG)
        oqc_ref[rot, :] = (y * scale_c).T.astype(oqc_ref.dtype)


def qprep(p, off, qg, cqg, wuq, tables, tm=256):
    m = p.shape[0]
    tm = min(tm, m)
    wq = GQA_HEADS * GQA_HEAD_DIM
    qb_blk = (off + P_QB) // wq
    cq_blk = (off + P_CQ) // MLA_Q_RANK
    assert (off + P_QB) % wq == 0 and (off + P_CQ) % MLA_Q_RANK == 0
    in_specs = [pl.BlockSpec((tm, wq), lambda i: (i, qb_blk)),
                pl.BlockSpec((tm, MLA_Q_RANK), lambda i: (i, cq_blk)),
                pl.BlockSpec((1, GQA_HEAD_DIM), lambda i: (0, 0)),
                pl.BlockSpec((1, MLA_Q_RANK), lambda i: (0, 0)),
                pl.BlockSpec(wuq.shape, lambda i: (0, 0))]
    args = [p, p, qg.reshape(1, -1), cqg.reshape(1, -1), wuq]
    if tables is not None:
        in_specs += [pl.BlockSpec((tm, LANES), lambda i: (i, 0))] * 4
        args += list(tables)
    wc = MLA_HEADS * MLA_QK_PAD
    return pl.pallas_call(
        functools.partial(_qprep_kernel, rope=tables is not None),
        grid=(m // tm,),
        in_specs=in_specs,
        out_specs=[pl.BlockSpec((wq, tm), lambda i: (0, i)), pl.BlockSpec((wc, tm), lambda i: (0, i))],
        out_shape=[jax.ShapeDtypeStruct((wq, m), BF16), jax.ShapeDtypeStruct((wc, m), BF16)],
        compiler_params=_params("parallel"),
        name="qprep",
    )(*args)


def _kvprep_kernel(kb_ref, vb_ref, ckv_ref, kr_ref, kg_ref, ckvg_ref, wk_ref, wv_ref, *rest, rope):
    if rope:
        cb_ref, sb_ref, cc_ref, sc_ref, okb_ref, okc_ref, ovb_ref, ovc_ref = rest
    else:
        okb_ref, okc_ref, ovb_ref, ovc_ref = rest
    kg = kg_ref[...]
    for h in range(GQA_KV_HEADS):
        sl = slice(h * GQA_HEAD_DIM, (h + 1) * GQA_HEAD_DIM)
        y = _head_rms(kb_ref[:, sl].astype(F32), kg)
        if rope:
            y = _rope(y, cb_ref[...], sb_ref[...], GQA_SEG)
        okb_ref[:, sl] = y.astype(okb_ref.dtype)
    ovb_ref[...] = vb_ref[...].astype(F32).T.astype(ovb_ref.dtype)
    ckv = _head_rms(ckv_ref[...].astype(F32), ckvg_ref[...]).astype(BF16)
    knope = jnp.dot(ckv, wk_ref[...], preferred_element_type=F32)
    ovc_ref[...] = jnp.dot(ckv, wv_ref[...], preferred_element_type=F32).T.astype(ovc_ref.dtype)
    kr = kr_ref[...].astype(F32)
    if rope:
        kr = _rope(kr, cc_ref[...], sc_ref[...], MLA_SEG)
    kr = kr.astype(okc_ref.dtype)
    for h in range(MLA_HEADS):
        okc_ref[:, h * MLA_QK_PAD:h * MLA_QK_PAD + MLA_NOPE] = (
            knope[:, h * MLA_NOPE:(h + 1) * MLA_NOPE].astype(okc_ref.dtype))
        okc_ref[:, h * MLA_QK_PAD + MLA_NOPE:(h + 1) * MLA_QK_PAD] = kr


def kvprep(p, off, kg, ckvg, wk, wv, tables):
    m = p.shape[0]
    tm = min(ATTN_TK, m)
    kb_blk = (off + P_KB) // GQA_KV_W
    vb_blk = (off + P_VB) // GQA_KV_W
    ckv_blk = (off + P_CKV) // MLA_KV_RANK
    kr_blk = (off + P_KR) // LANES
    assert (off + P_KB) % GQA_KV_W == 0 and (off + P_CKV) % MLA_KV_RANK == 0 and (off + P_KR) % LANES == 0
    in_specs = [pl.BlockSpec((tm, GQA_KV_W), lambda i: (i, kb_blk)),
                pl.BlockSpec((tm, GQA_KV_W), lambda i: (i, vb_blk)),
                pl.BlockSpec((tm, MLA_KV_RANK), lambda i: (i, ckv_blk)),
                pl.BlockSpec((tm, LANES), lambda i: (i, kr_blk)),
                pl.BlockSpec((1, GQA_HEAD_DIM), lambda i: (0, 0)),
                pl.BlockSpec((1, MLA_KV_RANK), lambda i: (0, 0)),
                pl.BlockSpec(wk.shape, lambda i: (0, 0)),
                pl.BlockSpec(wv.shape, lambda i: (0, 0))]
    args = [p, p, p, p, kg.reshape(1, -1), ckvg.reshape(1, -1), wk, wv]
    if tables is not None:
        in_specs += [pl.BlockSpec((tm, LANES), lambda i: (i, 0))] * 4
        args += list(tables)
    wkc = MLA_HEADS * MLA_QK_PAD
    wvc = MLA_HEADS * MLA_V
    n = m // tm
    return pl.pallas_call(
        functools.partial(_kvprep_kernel, rope=tables is not None),
        grid=(n,),
        in_specs=in_specs,
        out_specs=[pl.BlockSpec((tm, GQA_KV_W), lambda i: (i, 0)),
                   pl.BlockSpec((tm, wkc), lambda i: (i, 0)),
                   pl.BlockSpec((None, GQA_KV_W, tm), lambda i: (i, 0, 0)),
                   pl.BlockSpec((None, wvc, tm), lambda i: (i, 0, 0))],
        out_shape=[jax.ShapeDtypeStruct((m, GQA_KV_W), BF16),
                   jax.ShapeDtypeStruct((m, wkc), BF16),
                   jax.ShapeDtypeStruct((n, GQA_KV_W, tm), BF16),
                   jax.ShapeDtypeStruct((n, wvc, tm), BF16)],
        compiler_params=_params("parallel"),
        name="kvprep",
    )(*args)


def _attn_kernel(*refs, has_x):
    if has_x:
        qt_ref, kc_ref, vct_ref, kx_ref, vxt_ref, o_ref, s_ref = refs
        n, _, tk = vxt_ref.shape
    else:
        qt_ref, kc_ref, vct_ref, o_ref, s_ref = refs
        n, tk = 0, 0
    n_slot, _, r = s_ref.shape
    rows = [kc_ref.shape[0]] + [tk] * n
    items = [(b, g) for b in range(qt_ref.shape[1] // r) for g in range(n + 1)]

    def keys(g):
        return kc_ref[...] if g == 0 else kx_ref[(g - 1) * tk:g * tk, :]

    def values_t(g):
        return vct_ref[...] if g == 0 else vxt_ref[g - 1]

    def qk(t):
        b, g = items[t]
        st = jnp.dot(keys(g), qt_ref[:, b * r:(b + 1) * r], preferred_element_type=F32)
        s_ref[t % n_slot, :rows[g], :] = st
        return jnp.max(st, axis=0, keepdims=True)

    chunk_max = {t: qk(t) for t in range(min(ATTN_AHEAD, len(items)))}
    m = l = acc = None
    for t, (b, g) in enumerate(items):
        if t + ATTN_AHEAD < len(items):
            chunk_max[t + ATTN_AHEAD] = qk(t + ATTN_AHEAD)
        m_new = chunk_max.pop(t) if g == 0 else jnp.maximum(m, chunk_max.pop(t))
        pt = jnp.exp2(s_ref[t % n_slot, :rows[g], :] - m_new)
        p_sum = jnp.sum(pt, axis=0, keepdims=True)
        pv = jnp.dot(values_t(g), pt.astype(BF16), preferred_element_type=F32)
        if g == 0:
            l, acc = p_sum, pv
        else:
            alpha = jnp.exp2(m - m_new)
            l, acc = alpha * l + p_sum, alpha * acc + pv
        m = m_new
        if g == n:
            o_ref[b * r:(b + 1) * r, :] = (acc / l).T.astype(o_ref.dtype)


def attention(qt, ctx_kv, x_kv, *, n_heads, group, dq, dv, name="attn"):
    s_len = qt.shape[1]
    r = min(ATTN_R, s_len)
    tq = min(ATTN_BLOCKS * r, s_len)
    kc, vct = ctx_kv
    tc = kc.shape[0]
    in_specs = [pl.BlockSpec((dq, tq), lambda h, i: (h, i)),
                pl.BlockSpec((tc, dq), lambda h, i: (0, h // group)),
                pl.BlockSpec((None, dv, tc), lambda h, i: (0, h // group, 0))]
    args = [qt, kc, vct]
    n_chunk, rows = 1, tc
    if x_kv is not None:
        kx, vxt = x_kv
        in_specs += [pl.BlockSpec((kx.shape[0], dq), lambda h, i: (0, h // group)),
                     pl.BlockSpec((vxt.shape[0], dv, vxt.shape[2]), lambda h, i: (0, h // group, 0))]
        args += [kx, vxt]
        n_chunk, rows = 1 + vxt.shape[0], max(tc, vxt.shape[2])
    return pl.pallas_call(
        functools.partial(_attn_kernel, has_x=x_kv is not None),
        grid=(n_heads, s_len // tq),
        in_specs=in_specs,
        out_specs=pl.BlockSpec((tq, dv), lambda h, i: (i, h)),
        out_shape=jax.ShapeDtypeStruct((s_len, n_heads * dv), BF16),
        scratch_shapes=[pltpu.VMEM((min(ATTN_AHEAD + 1, n_chunk), rows, r), F32)],
        compiler_params=_params("parallel", "arbitrary"),
        name=name,
    )(*args)


def _conv_kernel(gb_ref, gc_ref, xa_ref, gcp_ref, xap_ref, gcn_ref, xan_ref, w_ref, o_ref):
    i = pl.program_id(0)
    tm = gb_ref.shape[0]
    u = gc_ref[...].astype(F32) * xa_ref[...].astype(F32)
    halo = gcp_ref.shape[0]
    u_prev = gcp_ref[halo - 1:halo, :].astype(F32) * xap_ref[halo - 1:halo, :].astype(F32)
    u_next = gcn_ref[0:1, :].astype(F32) * xan_ref[0:1, :].astype(F32)
    u_prev = jnp.where(i == 0, 0.0, u_prev)
    u_next = jnp.where(i == pl.num_programs(0) - 1, 0.0, u_next)
    row = lax.broadcasted_iota(jnp.int32, u.shape, 0)
    below = jnp.where(row == 0, u_prev, pltpu.roll(u, 1, 0))
    above = jnp.where(row == tm - 1, u_next, pltpu.roll(u, tm - 1, 0))
    w = w_ref[...]
    y = below * w[0:1, :] + u * w[1:2, :] + above * w[2:3, :]
    o_ref[...] = (gb_ref[...].astype(F32) * y).astype(o_ref.dtype)


def short_conv(p, off, w, tm=512):
    m = p.shape[0]
    tm = min(tm, m)
    c = CONV_W
    blk = (off + P_A) // c
    assert (off + P_A) % c == 0
    hb = tm // BF16_SUBLANES
    n_hb = m // BF16_SUBLANES
    main = lambda k: pl.BlockSpec((tm, c), lambda i: (i, blk + k))
    prev = lambda k: pl.BlockSpec((BF16_SUBLANES, c), lambda i: (jnp.maximum(i * hb - 1, 0), blk + k))
    nxt = lambda k: pl.BlockSpec((BF16_SUBLANES, c), lambda i: (jnp.minimum((i + 1) * hb, n_hb - 1), blk + k))
    return pl.pallas_call(
        _conv_kernel,
        grid=(m // tm,),
        in_specs=[main(0), main(1), main(2), prev(1), prev(2), nxt(1), nxt(2),
                  pl.BlockSpec((CONV_K, c), lambda i: (0, 0))],
        out_specs=pl.BlockSpec((tm, c), lambda i: (i, 0)),
        out_shape=jax.ShapeDtypeStruct((m, c), BF16),
        compiler_params=_params("parallel"),
        name="short_conv",
    )(p, p, p, p, p, p, p, w)


def _gelu_tanh(x):
    c = math.sqrt(2.0 / math.pi)
    return x * (0.5 * (1.0 + jnp.tanh(c * (x + 0.044715 * (x * x * x)))))


def _sgu_kernel(u_ref, v_ref, g_ref, b_ref, ws_ref, bs_ref, o_ref):
    tm = u_ref.shape[0]
    n_chunk = tm // SGU_CHUNK
    gw = SGU_W // SGU_GROUPS
    v = _gelu_tanh(v_ref[...].astype(F32))
    mu = jnp.mean(v, axis=-1, keepdims=True)
    vc = v - mu
    vn = vc * lax.rsqrt(jnp.mean(vc * vc, axis=-1, keepdims=True) + EPS) * g_ref[...] + b_ref[...]
    vn = vn.astype(BF16)
    for g in range(SGU_GROUPS):
        cols = slice(g * gw, (g + 1) * gw)
        rhs = jnp.concatenate([vn[c * SGU_CHUNK:(c + 1) * SGU_CHUNK, cols] for c in range(n_chunk)], axis=1)
        mixed = jnp.dot(ws_ref[g], rhs, preferred_element_type=F32)
        for c in range(n_chunk):
            rows = slice(c * SGU_CHUNK, (c + 1) * SGU_CHUNK)
            u = _gelu_tanh(u_ref[rows, cols].astype(F32))
            o_ref[rows, cols] = (u * (mixed[:, c * gw:(c + 1) * gw] + bs_ref[g])).astype(o_ref.dtype)


def sgu(p, off, ln_g, ln_b, w_s, b_s, tm=512):
    m = p.shape[0]
    tm = min(tm, m)
    c = SGU_W
    blk = (off + P_D) // c
    assert (off + P_D) % c == 0
    gw = SGU_W // SGU_GROUPS
    bs = jnp.broadcast_to(b_s[:, :, None], (SGU_GROUPS, SGU_CHUNK, gw))
    return pl.pallas_call(
        _sgu_kernel,
        grid=(m // tm,),
        in_specs=[pl.BlockSpec((tm, c), lambda i: (i, blk)),
                  pl.BlockSpec((tm, c), lambda i: (i, blk + 1)),
                  pl.BlockSpec((1, c), lambda i: (0, 0)),
                  pl.BlockSpec((1, c), lambda i: (0, 0)),
                  pl.BlockSpec(w_s.shape, lambda i: (0, 0, 0)),
                  pl.BlockSpec(bs.shape, lambda i: (0, 0, 0))],
        out_specs=pl.BlockSpec((tm, c), lambda i: (i, 0)),
        out_shape=jax.ShapeDtypeStruct((m, c), BF16),
        compiler_params=_params("parallel"),
        name="sgu",
    )(p, p, ln_g.reshape(1, c), ln_b.reshape(1, c), w_s, bs)


def _merge_kernel(*refs):
    ys, pgs, bgs, wbs, o_ref = refs[0:4], refs[4:8], refs[8:12], refs[12:16], refs[16]
    acc = None
    for i in range(N_BRANCH):
        z = pgs[i][...].astype(F32) + bgs[i][...]
        gate = 1.0 / (1.0 + jnp.exp(-z))
        term = gate * jnp.dot(ys[i][...], wbs[i][...], preferred_element_type=F32)
        acc = term if acc is None else acc + term
    o_ref[...] = acc.astype(o_ref.dtype)


def merge(ys, p, off, b_gate, w_branch, l, tm=1024, tn=512):
    m = p.shape[0]
    tm = min(tm, m)
    nj = D_MODEL // tn
    g_blk = (off + P_G) // tn
    assert (off + P_G) % tn == 0
    in_specs = [pl.BlockSpec((tm, BRANCH_W), lambda i, j: (i, 0))] * N_BRANCH
    in_specs += [pl.BlockSpec((tm, tn), lambda i, j, b=b: (i, g_blk + b * nj + j)) for b in range(N_BRANCH)]
    in_specs += [pl.BlockSpec((1, tn), lambda i, j, b=b: (0, b * nj + j)) for b in range(N_BRANCH)]
    in_specs += [pl.BlockSpec((None, None, BRANCH_W, tn), lambda i, j, b=b: (l, b, 0, j))
                 for b in range(N_BRANCH)]
    return pl.pallas_call(
        _merge_kernel,
        grid=(m // tm, nj),
        in_specs=in_specs,
        out_specs=pl.BlockSpec((tm, tn), lambda i, j: (i, j)),
        out_shape=jax.ShapeDtypeStruct((m, D_MODEL), BF16),
        compiler_params=_params("parallel", "arbitrary"),
        name="merge",
    )(*ys, p, p, p, p, *([b_gate.reshape(1, -1)] * N_BRANCH), *([w_branch] * N_BRANCH))


def _rope_tables(n_tok):
    t = jnp.arange(n_tok)
    row = (t // GRID_W).astype(F32)[:, None]
    col = (t % GRID_W).astype(F32)[:, None]

    def table(rot_dim, pad):
        axis_dim = rot_dim // 2
        inv = ROPE_BASE ** (-jnp.arange(0, axis_dim, 2, dtype=F32) / axis_dim)
        ar, ac = row * inv, col * inv
        cos = jnp.concatenate([jnp.cos(ar), jnp.cos(ar), jnp.cos(ac), jnp.cos(ac)], axis=1)
        sin = jnp.concatenate([-jnp.sin(ar), jnp.sin(ar), -jnp.sin(ac), jnp.sin(ac)], axis=1)
        if pad:
            cos = jnp.pad(cos, ((0, 0), (0, pad)))
            sin = jnp.pad(sin, ((0, 0), (0, pad)))
        return cos, sin

    cb, sb = table(GQA_HEAD_DIM, 0)
    cc, sc = table(MLA_ROPE, LANES - MLA_ROPE)
    return cb, sb, cc, sc


def _w_in_relayout_kernel(w_ref, o_ref):
    kv, q, a, d = KV_COLS, KV_COLS + Q_COLS, KV_COLS + Q_COLS + A_COLS, KV_COLS + Q_COLS + A_COLS + D_COLS
    for dst, lo, hi in ((P_G, d, d + G_COLS), (P_A, q, a), (P_D, a, d), (P_QB, kv, q), (P_KB, 0, kv)):
        o_ref[:, dst:dst + hi - lo] = w_ref[:, lo:hi].astype(o_ref.dtype)
    used = P_KB + KV_COLS
    o_ref[:, used:] = jnp.zeros((o_ref.shape[0], P_COLS - used), o_ref.dtype)


def w_in_relayout(w_in, tr=128):
    n_l, d, n = w_in.shape
    return pl.pallas_call(
        _w_in_relayout_kernel,
        grid=(n_l, d // tr),
        in_specs=[pl.BlockSpec((None, tr, n), lambda l, i: (l, i, 0))],
        out_specs=pl.BlockSpec((None, tr, P_COLS), lambda l, i: (l, i, 0)),
        out_shape=jax.ShapeDtypeStruct((n_l, d, P_COLS), BF16),
        compiler_params=_params("parallel", "parallel"),
        name="w_in_relayout",
    )(w_in)


def _prep_weights(w_in, w_uq, w_ukv, w_branch, w_ff2):
    n_l = w_in.shape[0]
    lane_pad = -w_in.shape[2] % LANES
    w_in_p = w_in_relayout(jnp.pad(w_in.astype(BF16), ((0, 0), (0, 0), (0, lane_pad))), tr=256)
    wuq = w_uq.reshape(n_l, MLA_Q_RANK, MLA_HEADS, MLA_NOPE + MLA_ROPE)
    wuq = jnp.pad(wuq, ((0, 0), (0, 0), (0, 0), (0, MLA_QK_PAD - MLA_NOPE - MLA_ROPE)))
    wuq = wuq.reshape(n_l, MLA_Q_RANK, MLA_HEADS * MLA_QK_PAD).astype(BF16)
    wukv = w_ukv.reshape(n_l, MLA_KV_RANK, MLA_HEADS, MLA_NOPE + MLA_V)
    wk = wukv[:, :, :, :MLA_NOPE].reshape(n_l, MLA_KV_RANK, MLA_HEADS * MLA_NOPE).astype(BF16)
    wv = wukv[:, :, :, MLA_NOPE:].reshape(n_l, MLA_KV_RANK, MLA_HEADS * MLA_V).astype(BF16)
    return dict(w_in=w_in_p, wuq=wuq, wk=wk, wv=wv, w_branch=w_branch.astype(BF16), w_ff2=w_ff2.astype(BF16))


def _kv_of(p, off, lw, lp, tables):
    return kvprep(p, off, lp["k_norm_g"], lp["mla_kv_norm_g"], lw["wk"], lw["wv"], tables)


def _mix(p, lw, lp, tables, ctx_kv, x_kv):
    qbt, qct = qprep(p, 0, lp["q_norm_g"], lp["mla_q_norm_g"], lw["wuq"], tables)
    gqa = lambda kv: None if kv is None else (kv[0], kv[2])
    mla = lambda kv: None if kv is None else (kv[1], kv[3])
    y_b = attention(qbt, gqa(ctx_kv), gqa(x_kv), n_heads=GQA_HEADS, group=GQA_GROUP, dq=GQA_HEAD_DIM,
                    dv=GQA_HEAD_DIM, name="attn_gqa")
    y_c = attention(qct, mla(ctx_kv), mla(x_kv), n_heads=MLA_HEADS, group=1, dq=MLA_QK_PAD, dv=MLA_V,
                    name="attn_mla")
    y_a = short_conv(p, 0, lp["conv_w"])
    y_d = sgu(p, 0, lp["sgu_ln_g"], lp["sgu_ln_b"], lw["sgu_w_s"], lp["sgu_b_s"])
    return merge([y_a, y_b, y_c, y_d], p, 0, lp["b_gate"], lw["w_branch"], lw["l"])


def kernel(x, c, ctx, c_ctx, w_ada, b_ada, norm_mix_g, w_in, b_gate, conv_w, q_norm_g, k_norm_g,
           mla_q_norm_g, mla_kv_norm_g, w_uq, w_ukv, sgu_ln_g, sgu_ln_b, sgu_w_s, sgu_b_s,
           w_branch, w_out, norm_ffn_g, w_ff1, w_ff2, final_norm_g):
    assert x.shape[0] == 1 and c.shape[0] == 1 and ctx.shape[0] == 1
    xs, zs = x[0], ctx[0]
    tables = _rope_tables(xs.shape[0])
    cond_t = jnp.stack([c[0], c_ctx], axis=1)
    X, Z = 0, 1

    wts = _prep_weights(w_in, w_uq, w_ukv, w_branch, w_ff2)
    b_ada3 = b_ada[:, None, :]

    def ffn(s, mod, row, l):
        h = norm_mod(s, norm_ffn_g[l], mod, row=row, k_shift=3, k_scale=4)
        u = matmul(h, w_ff1, l, epilogue="relu2", name="ffn_up")
        return matmul(u, wts["w_ff2"], l, out_dtype=F32, epilogue="residual", res=s, mod=mod, row=row,
                      k_gate=5, tm=512, tn=512, name="ffn_down")

    def out_proj(merged, s, mod, row, l):
        return matmul(merged, w_out, l, out_dtype=F32, epilogue="residual", res=s, mod=mod, row=row,
                      k_gate=2, tn=512, name="out_proj")

    for l in range(DEPTH):
        last = l == DEPTH - 1
        lw = dict(l=l, wuq=wts["wuq"][l], wk=wts["wk"][l], wv=wts["wv"][l], w_branch=wts["w_branch"],
                  sgu_w_s=sgu_w_s[l].astype(BF16))
        lp = dict(b_gate=b_gate[l], conv_w=conv_w[l], q_norm_g=q_norm_g[l], k_norm_g=k_norm_g[l],
                  mla_q_norm_g=mla_q_norm_g[l], mla_kv_norm_g=mla_kv_norm_g[l],
                  sgu_ln_g=sgu_ln_g[l], sgu_ln_b=sgu_ln_b[l], sgu_b_s=sgu_b_s[l])
        mod = adaln(cond_t, w_ada, b_ada3, l)

        hz = norm_mod(zs, norm_mix_g[l], mod, row=Z, k_shift=0, k_scale=1)
        if last:
            kv_w = P_COLS - P_KV_START
            pz = matmul(hz, wts["w_in"], l, n=kv_w, col0=P_KV_START // 512, tn=512, name="in_proj_ctx_kv")
            kv_z = _kv_of(pz, -P_KV_START, lw, lp, None)
        else:
            pz = matmul(hz, wts["w_in"], l, name="in_proj")
            kv_z = _kv_of(pz, 0, lw, lp, None)

        hx = norm_mod(xs, norm_mix_g[l], mod, row=X, k_shift=0, k_scale=1)
        px = matmul(hx, wts["w_in"], l, name="in_proj")
        kv_x = _kv_of(px, 0, lw, lp, tables)
        xs = out_proj(_mix(px, lw, lp, tables, kv_z, kv_x), xs, mod, X, l)
        xs = ffn(xs, mod, X, l)

        if not last:
            zs = out_proj(_mix(pz, lw, lp, None, kv_z, None), zs, mod, Z, l)
            zs = ffn(zs, mod, Z, l)

    out = norm_mod(xs, final_norm_g, out_dtype=F32)
    return out[None]
```

```python
import functools
import math

import jax
import jax.numpy as jnp
from jax import lax
from jax.experimental import pallas as pl
from jax.experimental.pallas import tpu as pltpu

F32 = jnp.float32
BF16 = jnp.bfloat16

D_MODEL = 2048
DEPTH = 2
GRID_W = 64
ROPE_BASE = 10000.0
EPS = 1e-6
N_BRANCH = 4
BRANCH_W = 1024
CONV_W = 1024
CONV_K = 3
GQA_HEADS = 8
GQA_KV_HEADS = 2
GQA_GROUP = GQA_HEADS // GQA_KV_HEADS
GQA_HEAD_DIM = 128
GQA_KV_W = GQA_KV_HEADS * GQA_HEAD_DIM
MLA_HEADS = 8
MLA_Q_RANK = 512
MLA_KV_RANK = 512
MLA_NOPE = 128
MLA_ROPE = 64
MLA_V = 128
SGU_W = 1024
SGU_GROUPS = 8
SGU_CHUNK = 128
D_FF = 4 * D_MODEL

KV_COLS = 2 * GQA_KV_W + MLA_KV_RANK + MLA_ROPE
Q_COLS = GQA_HEADS * GQA_HEAD_DIM + MLA_Q_RANK
A_COLS = 3 * CONV_W
D_COLS = 2 * SGU_W
G_COLS = N_BRANCH * D_MODEL

LANES = 128
BF16_SUBLANES = 16
MLA_QK_PAD = 256
ATTN_TK = 512
ATTN_R = 512
ATTN_BLOCKS = 2
ATTN_AHEAD = 3
VMEM_LIMIT_BYTES = 48 * 1024 * 1024

P_TILE = 512
P_QB = 0
P_CQ = P_QB + GQA_HEADS * GQA_HEAD_DIM
P_A = P_CQ + MLA_Q_RANK
P_D = P_A + A_COLS
P_G = P_D + D_COLS
P_COLS = P_G + G_COLS
PK_KB = 0
PK_VB = PK_KB + GQA_KV_W
PK_CKV = PK_VB + GQA_KV_W
PK_KR = PK_CKV + MLA_KV_RANK
PK_COLS = PK_KR + LANES


def _params(*sem):
    return pltpu.CompilerParams(dimension_semantics=sem, vmem_limit_bytes=VMEM_LIMIT_BYTES)


def _adaln_kernel(ct_ref, w_ref, b_ref, o_ref):
    ct = ct_ref[...]
    s = ct / (1.0 + jnp.exp(-ct))
    w = w_ref[...]
    for r in range(2):
        o_ref[r:r + 1, :] = jnp.sum(w * s[:, r:r + 1], axis=0, keepdims=True) + b_ref[...]


def adaln(cond_t, w, b, l, tn=512):
    _, d, n = w.shape
    return pl.pallas_call(
        _adaln_kernel,
        grid=(n // tn,),
        in_specs=[pl.BlockSpec((d, 2), lambda j: (0, 0)),
                  pl.BlockSpec((None, d, tn), lambda j: (l, 0, j)),
                  pl.BlockSpec((None, 1, tn), lambda j: (l, 0, j))],
        out_specs=pl.BlockSpec((2, tn), lambda j: (0, j)),
        out_shape=jax.ShapeDtypeStruct((2, n), F32),
        compiler_params=_params("arbitrary"),
        name="adaln",
    )(cond_t, w, b)


def _norm_kernel(x_ref, g_ref, *rest, row):
    o_ref = rest[-1]
    x = x_ref[...]
    y = x * lax.rsqrt(jnp.mean(x * x, axis=-1, keepdims=True) + EPS) * g_ref[...]
    if len(rest) == 3:
        sh_ref, sc_ref = rest[0], rest[1]
        y = y * (1.0 + sc_ref[row:row + 1, :]) + sh_ref[row:row + 1, :]
    o_ref[...] = y.astype(o_ref.dtype)


def norm_mod(x, g, mod=None, row=0, k_shift=0, k_scale=1, out_dtype=BF16, tm=512):
    m, d = x.shape
    tm = min(tm, m)
    in_specs = [pl.BlockSpec((tm, d), lambda i: (i, 0)), pl.BlockSpec((1, d), lambda i: (0, 0))]
    args = [x, g.reshape(1, d)]
    if mod is not None:
        in_specs += [pl.BlockSpec((2, d), lambda i: (0, k_shift)), pl.BlockSpec((2, d), lambda i: (0, k_scale))]
        args += [mod, mod]
    return pl.pallas_call(
        functools.partial(_norm_kernel, row=row),
        grid=(m // tm,),
        in_specs=in_specs,
        out_specs=pl.BlockSpec((tm, d), lambda i: (i, 0)),
        out_shape=jax.ShapeDtypeStruct((m, d), out_dtype),
        compiler_params=_params("parallel"),
        name="norm_mod",
    )(*args)


def _mm_kernel(a_ref, w_ref, *rest, epilogue, row, cast_w):
    if cast_w:
        wb_ref = rest[-1]
        rest = rest[:-1]

        @pl.when(pl.program_id(1) == 0)
        def _():
            wb_ref[...] = w_ref[...].astype(BF16)

        w = wb_ref[...]
    else:
        w = w_ref[...]
    o_ref = rest[-1]
    acc = jnp.dot(a_ref[...], w, preferred_element_type=F32)
    if epilogue == "relu2":
        r = jnp.maximum(acc, 0.0)
        acc = r * r
    elif epilogue == "residual":
        res_ref, gate_ref = rest[0], rest[1]
        acc = res_ref[...] + gate_ref[row:row + 1, :] * acc
    o_ref[...] = acc.astype(o_ref.dtype)


def matmul(a, w, l, *, out_dtype=BF16, epilogue="none", res=None, mod=None, row=0, k_gate=0,
           tm=1024, tn=1024, name="matmul"):
    m, k = a.shape
    n = w.shape[2]
    tm, tn = min(tm, m), min(tn, n)
    assert m % tm == 0 and n % tn == 0
    cast_w = w.dtype == F32
    order = (lambda f: (lambda j, i: f(i, j))) if cast_w else (lambda f: f)
    in_specs = [pl.BlockSpec((tm, k), order(lambda i, j: (i, 0))),
                pl.BlockSpec((None, k, tn), order(lambda i, j: (l, 0, j)))]
    args = [a, w]
    if epilogue == "residual":
        kb = k_gate * (n // tn)
        in_specs += [pl.BlockSpec((tm, tn), order(lambda i, j: (i, j))),
                     pl.BlockSpec((2, tn), order(lambda i, j: (0, kb + j)))]
        args += [res, mod]
    grid = (n // tn, m // tm) if cast_w else (m // tm, n // tn)
    return pl.pallas_call(
        functools.partial(_mm_kernel, epilogue=epilogue, row=row, cast_w=cast_w),
        grid=grid,
        in_specs=in_specs,
        out_specs=pl.BlockSpec((tm, tn), order(lambda i, j: (i, j))),
        out_shape=jax.ShapeDtypeStruct((m, n), out_dtype),
        scratch_shapes=[pltpu.VMEM((k, tn), BF16)] if cast_w else [],
        compiler_params=_params("parallel", "arbitrary"),
        name=name,
    )(*args)


def _rope(y, cos, sin, seg):
    lane = lax.broadcasted_iota(jnp.int32, y.shape, 1)
    first = (lane // seg) % 2 == 0
    partner = jnp.where(first, pltpu.roll(y, LANES - seg, 1), pltpu.roll(y, seg, 1))
    return y * cos + partner * sin


def _head_rms(x, g):
    return x * lax.rsqrt(jnp.mean(x * x, axis=-1, keepdims=True) + EPS) * g


LOG2_E = math.log2(math.e)
GQA_SEG = GQA_HEAD_DIM // 4
MLA_SEG = MLA_ROPE // 4


def _qprep_kernel(qb_ref, cq_ref, qg_ref, cqg_ref, wuq_ref, *rest, rope):
    if rope:
        cb_ref, sb_ref, cc_ref, sc_ref, oqb_ref, oqc_ref = rest
    else:
        oqb_ref, oqc_ref = rest
    qg = qg_ref[...]
    scale_b = LOG2_E * GQA_HEAD_DIM ** -0.5
    for h in range(GQA_HEADS):
        sl = slice(h * GQA_HEAD_DIM, (h + 1) * GQA_HEAD_DIM)
        y = _head_rms(qb_ref[:, sl].astype(F32), qg)
        if rope:
            y = _rope(y, cb_ref[...], sb_ref[...], GQA_SEG)
        oqb_ref[sl, :] = (y * scale_b).T.astype(oqb_ref.dtype)
    cq = _head_rms(cq_ref[...].astype(F32), cqg_ref[...]).astype(BF16)
    qc = jnp.dot(cq, wuq_ref[...], preferred_element_type=F32)
    scale_c = LOG2_E * (MLA_NOPE + MLA_ROPE) ** -0.5
    for h in range(MLA_HEADS):
        nope = slice(h * MLA_QK_PAD, h * MLA_QK_PAD + MLA_NOPE)
        rot = slice(h * MLA_QK_PAD + MLA_NOPE, (h + 1) * MLA_QK_PAD)
        oqc_ref[nope, :] = (qc[:, nope] * scale_c).T.astype(oqc_ref.dtype)
        y = qc[:, rot]
        if rope:
            y = _rope(y, cc_ref[...], sc_ref[...], MLA_SEG)
        oqc_ref[rot, :] = (y * scale_c).T.astype(oqc_ref.dtype)


def qprep(p, qg, cqg, wuq, tables, tm=256):
    m = p.shape[0]
    tm = min(tm, m)
    wq = GQA_HEADS * GQA_HEAD_DIM
    qb_blk = P_QB // wq
    cq_blk = P_CQ // MLA_Q_RANK
    assert P_QB % wq == 0 and P_CQ % MLA_Q_RANK == 0
    in_specs = [pl.BlockSpec((tm, wq), lambda i: (i, qb_blk)),
                pl.BlockSpec((tm, MLA_Q_RANK), lambda i: (i, cq_blk)),
                pl.BlockSpec((1, GQA_HEAD_DIM), lambda i: (0, 0)),
                pl.BlockSpec((1, MLA_Q_RANK), lambda i: (0, 0)),
                pl.BlockSpec(wuq.shape, lambda i: (0, 0))]
    args = [p, p, qg.reshape(1, -1), cqg.reshape(1, -1), wuq]
    if tables is not None:
        in_specs += [pl.BlockSpec((tm, LANES), lambda i: (i, 0))] * 4
        args += list(tables)
    wc = MLA_HEADS * MLA_QK_PAD
    return pl.pallas_call(
        functools.partial(_qprep_kernel, rope=tables is not None),
        grid=(m // tm,),
        in_specs=in_specs,
        out_specs=[pl.BlockSpec((wq, tm), lambda i: (0, i)), pl.BlockSpec((wc, tm), lambda i: (0, i))],
        out_shape=[jax.ShapeDtypeStruct((wq, m), BF16), jax.ShapeDtypeStruct((wc, m), BF16)],
        compiler_params=_params("parallel"),
        name="qprep",
    )(*args)


def _kvprep_kernel(kb_ref, vb_ref, ckv_ref, kr_ref, kg_ref, ckvg_ref, wk_ref, wv_ref, *rest, rope):
    if rope:
        cb_ref, sb_ref, cc_ref, sc_ref, okb_ref, okc_ref, ovb_ref, ovc_ref = rest
    else:
        okb_ref, okc_ref, ovb_ref, ovc_ref = rest
    kg = kg_ref[...]
    for h in range(GQA_KV_HEADS):
        sl = slice(h * GQA_HEAD_DIM, (h + 1) * GQA_HEAD_DIM)
        y = _head_rms(kb_ref[:, sl].astype(F32), kg)
        if rope:
            y = _rope(y, cb_ref[...], sb_ref[...], GQA_SEG)
        okb_ref[:, sl] = y.astype(okb_ref.dtype)
    ovb_ref[...] = vb_ref[...].astype(F32).T.astype(ovb_ref.dtype)
    ckv = _head_rms(ckv_ref[...].astype(F32), ckvg_ref[...]).astype(BF16)
    knope = jnp.dot(ckv, wk_ref[...], preferred_element_type=F32)
    ovc_ref[...] = jnp.dot(ckv, wv_ref[...], preferred_element_type=F32).T.astype(ovc_ref.dtype)
    kr = kr_ref[...].astype(F32)
    if rope:
        kr = _rope(kr, cc_ref[...], sc_ref[...], MLA_SEG)
    kr = kr.astype(okc_ref.dtype)
    for h in range(MLA_HEADS):
        okc_ref[:, h * MLA_QK_PAD:h * MLA_QK_PAD + MLA_NOPE] = (
            knope[:, h * MLA_NOPE:(h + 1) * MLA_NOPE].astype(okc_ref.dtype))
        okc_ref[:, h * MLA_QK_PAD + MLA_NOPE:(h + 1) * MLA_QK_PAD] = kr


def kvprep(p, kg, ckvg, wk, wv, tables):
    m = p.shape[0]
    tm = min(ATTN_TK, m)
    kb_blk = PK_KB // GQA_KV_W
    vb_blk = PK_VB // GQA_KV_W
    ckv_blk = PK_CKV // MLA_KV_RANK
    kr_blk = PK_KR // LANES
    assert PK_KB % GQA_KV_W == 0 and PK_VB % GQA_KV_W == 0 and PK_CKV % MLA_KV_RANK == 0 and PK_KR % LANES == 0
    in_specs = [pl.BlockSpec((tm, GQA_KV_W), lambda i: (i, kb_blk)),
                pl.BlockSpec((tm, GQA_KV_W), lambda i: (i, vb_blk)),
                pl.BlockSpec((tm, MLA_KV_RANK), lambda i: (i, ckv_blk)),
                pl.BlockSpec((tm, LANES), lambda i: (i, kr_blk)),
                pl.BlockSpec((1, GQA_HEAD_DIM), lambda i: (0, 0)),
                pl.BlockSpec((1, MLA_KV_RANK), lambda i: (0, 0)),
                pl.BlockSpec(wk.shape, lambda i: (0, 0)),
                pl.BlockSpec(wv.shape, lambda i: (0, 0))]
    args = [p, p, p, p, kg.reshape(1, -1), ckvg.reshape(1, -1), wk, wv]
    if tables is not None:
        in_specs += [pl.BlockSpec((tm, LANES), lambda i: (i, 0))] * 4
        args += list(tables)
    wkc = MLA_HEADS * MLA_QK_PAD
    wvc = MLA_HEADS * MLA_V
    n = m // tm
    return pl.pallas_call(
        functools.partial(_kvprep_kernel, rope=tables is not None),
        grid=(n,),
        in_specs=in_specs,
        out_specs=[pl.BlockSpec((tm, GQA_KV_W), lambda i: (i, 0)),
                   pl.BlockSpec((tm, wkc), lambda i: (i, 0)),
                   pl.BlockSpec((None, GQA_KV_W, tm), lambda i: (i, 0, 0)),
                   pl.BlockSpec((None, wvc, tm), lambda i: (i, 0, 0))],
        out_shape=[jax.ShapeDtypeStruct((m, GQA_KV_W), BF16),
                   jax.ShapeDtypeStruct((m, wkc), BF16),
                   jax.ShapeDtypeStruct((n, GQA_KV_W, tm), BF16),
                   jax.ShapeDtypeStruct((n, wvc, tm), BF16)],
        compiler_params=_params("parallel"),
        name="kvprep",
    )(*args)


def _attn_kernel(*refs, has_x):
    if has_x:
        qt_ref, kc_ref, vct_ref, kx_ref, vxt_ref, o_ref, s_ref = refs
        n, _, tk = vxt_ref.shape
    else:
        qt_ref, kc_ref, vct_ref, o_ref, s_ref = refs
        n, tk = 0, 0
    n_slot, _, r = s_ref.shape
    rows = [kc_ref.shape[0]] + [tk] * n
    items = [(b, g) for b in range(qt_ref.shape[1] // r) for g in range(n + 1)]

    def keys(g):
        return kc_ref[...] if g == 0 else kx_ref[(g - 1) * tk:g * tk, :]

    def values_t(g):
        return vct_ref[...] if g == 0 else vxt_ref[g - 1]

    def qk(t):
        b, g = items[t]
        st = jnp.dot(keys(g), qt_ref[:, b * r:(b + 1) * r], preferred_element_type=F32)
        s_ref[t % n_slot, :rows[g], :] = st
        return jnp.max(st, axis=0, keepdims=True)

    chunk_max = {t: qk(t) for t in range(min(ATTN_AHEAD, len(items)))}
    m = l = acc = None
    for t, (b, g) in enumerate(items):
        if t + ATTN_AHEAD < len(items):
            chunk_max[t + ATTN_AHEAD] = qk(t + ATTN_AHEAD)
        m_new = chunk_max.pop(t) if g == 0 else jnp.maximum(m, chunk_max.pop(t))
        pt = jnp.exp2(s_ref[t % n_slot, :rows[g], :] - m_new)
        p_sum = jnp.sum(pt, axis=0, keepdims=True)
        pv = jnp.dot(values_t(g), pt.astype(BF16), preferred_element_type=F32)
        if g == 0:
            l, acc = p_sum, pv
        else:
            alpha = jnp.exp2(m - m_new)
            l, acc = alpha * l + p_sum, alpha * acc + pv
        m = m_new
        if g == n:
            o_ref[b * r:(b + 1) * r, :] = (acc / l).T.astype(o_ref.dtype)


def attention(qt, ctx_kv, x_kv, *, n_heads, group, dq, dv, name="attn"):
    s_len = qt.shape[1]
    r = min(ATTN_R, s_len)
    tq = min(ATTN_BLOCKS * r, s_len)
    kc, vct = ctx_kv
    tc = kc.shape[0]
    in_specs = [pl.BlockSpec((dq, tq), lambda h, i: (h, i)),
                pl.BlockSpec((tc, dq), lambda h, i: (0, h // group)),
                pl.BlockSpec((None, dv, tc), lambda h, i: (0, h // group, 0))]
    args = [qt, kc, vct]
    n_chunk, rows = 1, tc
    if x_kv is not None:
        kx, vxt = x_kv
        in_specs += [pl.BlockSpec((kx.shape[0], dq), lambda h, i: (0, h // group)),
                     pl.BlockSpec((vxt.shape[0], dv, vxt.shape[2]), lambda h, i: (0, h // group, 0))]
        args += [kx, vxt]
        n_chunk, rows = 1 + vxt.shape[0], max(tc, vxt.shape[2])
    return pl.pallas_call(
        functools.partial(_attn_kernel, has_x=x_kv is not None),
        grid=(n_heads, s_len // tq),
        in_specs=in_specs,
        out_specs=pl.BlockSpec((tq, dv), lambda h, i: (i, h)),
        out_shape=jax.ShapeDtypeStruct((s_len, n_heads * dv), BF16),
        scratch_shapes=[pltpu.VMEM((min(ATTN_AHEAD + 1, n_chunk), rows, r), F32)],
        compiler_params=_params("parallel", "arbitrary"),
        name=name,
    )(*args)


def _conv_kernel(gb_ref, gc_ref, xa_ref, gcp_ref, xap_ref, gcn_ref, xan_ref, w_ref, o_ref):
    i = pl.program_id(0)
    tm = gb_ref.shape[0]
    u = gc_ref[...].astype(F32) * xa_ref[...].astype(F32)
    halo = gcp_ref.shape[0]
    u_prev = gcp_ref[halo - 1:halo, :].astype(F32) * xap_ref[halo - 1:halo, :].astype(F32)
    u_next = gcn_ref[0:1, :].astype(F32) * xan_ref[0:1, :].astype(F32)
    u_prev = jnp.where(i == 0, 0.0, u_prev)
    u_next = jnp.where(i == pl.num_programs(0) - 1, 0.0, u_next)
    row = lax.broadcasted_iota(jnp.int32, u.shape, 0)
    below = jnp.where(row == 0, u_prev, pltpu.roll(u, 1, 0))
    above = jnp.where(row == tm - 1, u_next, pltpu.roll(u, tm - 1, 0))
    w = w_ref[...]
    y = below * w[0:1, :] + u * w[1:2, :] + above * w[2:3, :]
    o_ref[...] = (gb_ref[...].astype(F32) * y).astype(o_ref.dtype)


def short_conv(p, w, tm=512):
    m = p.shape[0]
    tm = min(tm, m)
    c = P_TILE
    nc = CONV_W // c
    blk = P_A // c
    assert P_A % c == 0 and CONV_W % c == 0
    hb = tm // BF16_SUBLANES
    n_hb = m // BF16_SUBLANES
    main = lambda k: pl.BlockSpec((tm, c), lambda i, j: (i, blk + k * nc + j))
    prev = lambda k: pl.BlockSpec((BF16_SUBLANES, c),
                                  lambda i, j: (jnp.maximum(i * hb - 1, 0), blk + k * nc + j))
    nxt = lambda k: pl.BlockSpec((BF16_SUBLANES, c),
                                 lambda i, j: (jnp.minimum((i + 1) * hb, n_hb - 1), blk + k * nc + j))
    return pl.pallas_call(
        _conv_kernel,
        grid=(m // tm, nc),
        in_specs=[main(0), main(1), main(2), prev(1), prev(2), nxt(1), nxt(2),
                  pl.BlockSpec((CONV_K, c), lambda i, j: (0, j))],
        out_specs=pl.BlockSpec((tm, c), lambda i, j: (i, j)),
        out_shape=jax.ShapeDtypeStruct((m, CONV_W), BF16),
        compiler_params=_params("parallel", "parallel"),
        name="short_conv",
    )(p, p, p, p, p, p, p, w)


def _gelu_tanh(x):
    c = math.sqrt(2.0 / math.pi)
    return x * (0.5 * (1.0 + jnp.tanh(c * (x + 0.044715 * (x * x * x)))))


def _sgu_kernel(*refs):
    nb = SGU_W // P_TILE
    u_refs, v_refs = refs[:nb], refs[nb:2 * nb]
    g_ref, b_ref, ws_ref, bs_ref, o_ref = refs[2 * nb:]
    tm = o_ref.shape[0]
    n_chunk = tm // SGU_CHUNK
    gw = SGU_W // SGU_GROUPS
    v = _gelu_tanh(jnp.concatenate([r[...] for r in v_refs], axis=1).astype(F32))
    mu = jnp.mean(v, axis=-1, keepdims=True)
    vc = v - mu
    vn = vc * lax.rsqrt(jnp.mean(vc * vc, axis=-1, keepdims=True) + EPS) * g_ref[...] + b_ref[...]
    vn = vn.astype(BF16)
    for g in range(SGU_GROUPS):
        cols = slice(g * gw, (g + 1) * gw)
        rhs = jnp.concatenate([vn[c * SGU_CHUNK:(c + 1) * SGU_CHUNK, cols] for c in range(n_chunk)], axis=1)
        mixed = jnp.dot(ws_ref[g], rhs, preferred_element_type=F32)
        for c in range(n_chunk):
            rows = slice(c * SGU_CHUNK, (c + 1) * SGU_CHUNK)
            u_ref, u0 = u_refs[g * gw // P_TILE], g * gw % P_TILE
            u = _gelu_tanh(u_ref[rows, u0:u0 + gw].astype(F32))
            o_ref[rows, cols] = (u * (mixed[:, c * gw:(c + 1) * gw] + bs_ref[g])).astype(o_ref.dtype)


def sgu(p, ln_g, ln_b, w_s, b_s, tm=512):
    m = p.shape[0]
    tm = min(tm, m)
    c = SGU_W
    nb = c // P_TILE
    blk = P_D // P_TILE
    assert P_D % P_TILE == 0 and c % P_TILE == 0
    gw = SGU_W // SGU_GROUPS
    bs = jnp.broadcast_to(b_s[:, :, None], (SGU_GROUPS, SGU_CHUNK, gw))
    uv_specs = [pl.BlockSpec((tm, P_TILE), lambda i, k=k: (i, blk + k)) for k in range(2 * nb)]
    return pl.pallas_call(
        _sgu_kernel,
        grid=(m // tm,),
        in_specs=uv_specs + [
                  pl.BlockSpec((1, c), lambda i: (0, 0)),
                  pl.BlockSpec((1, c), lambda i: (0, 0)),
                  pl.BlockSpec(w_s.shape, lambda i: (0, 0, 0)),
                  pl.BlockSpec(bs.shape, lambda i: (0, 0, 0))],
        out_specs=pl.BlockSpec((tm, c), lambda i: (i, 0)),
        out_shape=jax.ShapeDtypeStruct((m, c), BF16),
        compiler_params=_params("parallel"),
        name="sgu",
    )(*([p] * (2 * nb)), ln_g.reshape(1, c), ln_b.reshape(1, c), w_s, bs)


def _merge_kernel(*refs):
    ys, pgs, bgs, wbs, o_ref = refs[0:4], refs[4:8], refs[8:12], refs[12:16], refs[16]
    acc = None
    for i in range(N_BRANCH):
        z = pgs[i][...].astype(F32) + bgs[i][...]
        gate = 1.0 / (1.0 + jnp.exp(-z))
        term = gate * jnp.dot(ys[i][...], wbs[i][...], preferred_element_type=F32)
        acc = term if acc is None else acc + term
    o_ref[...] = acc.astype(o_ref.dtype)


def merge(ys, p, b_gate, w_branch, l, tm=1024, tn=P_TILE):
    m = p.shape[0]
    tm = min(tm, m)
    nj = D_MODEL // tn
    g_blk = P_G // tn
    assert P_G % tn == 0
    in_specs = [pl.BlockSpec((tm, BRANCH_W), lambda i, j: (i, 0))] * N_BRANCH
    in_specs += [pl.BlockSpec((tm, tn), lambda i, j, b=b: (i, g_blk + b * nj + j)) for b in range(N_BRANCH)]
    in_specs += [pl.BlockSpec((1, tn), lambda i, j, b=b: (0, b * nj + j)) for b in range(N_BRANCH)]
    in_specs += [pl.BlockSpec((None, None, BRANCH_W, tn), lambda i, j, b=b: (l, b, 0, j))
                 for b in range(N_BRANCH)]
    return pl.pallas_call(
        _merge_kernel,
        grid=(m // tm, nj),
        in_specs=in_specs,
        out_specs=pl.BlockSpec((tm, tn), lambda i, j: (i, j)),
        out_shape=jax.ShapeDtypeStruct((m, D_MODEL), BF16),
        compiler_params=_params("parallel", "arbitrary"),
        name="merge",
    )(*ys, p, p, p, p, *([b_gate.reshape(1, -1)] * N_BRANCH), *([w_branch] * N_BRANCH))


def _rope_tables(n_tok):
    t = jnp.arange(n_tok)
    row = (t // GRID_W).astype(F32)[:, None]
    col = (t % GRID_W).astype(F32)[:, None]

    def table(rot_dim, pad):
        axis_dim = rot_dim // 2
        inv = ROPE_BASE ** (-jnp.arange(0, axis_dim, 2, dtype=F32) / axis_dim)
        ar, ac = row * inv, col * inv
        cos = jnp.concatenate([jnp.cos(ar), jnp.cos(ar), jnp.cos(ac), jnp.cos(ac)], axis=1)
        sin = jnp.concatenate([-jnp.sin(ar), jnp.sin(ar), -jnp.sin(ac), jnp.sin(ac)], axis=1)
        if pad:
            cos = jnp.pad(cos, ((0, 0), (0, pad)))
            sin = jnp.pad(sin, ((0, 0), (0, pad)))
        return cos, sin

    cb, sb = table(GQA_HEAD_DIM, 0)
    cc, sc = table(MLA_ROPE, LANES - MLA_ROPE)
    return cb, sb, cc, sc


def _prep_weights(w_in, w_uq, w_ukv, w_branch, w_ff2):
    n_l = w_in.shape[0]
    w_main = w_in[:, :, KV_COLS:].astype(BF16)
    w_kv = jnp.pad(w_in[:, :, :KV_COLS].astype(BF16), ((0, 0), (0, 0), (0, PK_COLS - KV_COLS)))
    wuq = w_uq.reshape(n_l, MLA_Q_RANK, MLA_HEADS, MLA_NOPE + MLA_ROPE)
    wuq = jnp.pad(wuq, ((0, 0), (0, 0), (0, 0), (0, MLA_QK_PAD - MLA_NOPE - MLA_ROPE)))
    wuq = wuq.reshape(n_l, MLA_Q_RANK, MLA_HEADS * MLA_QK_PAD).astype(BF16)
    wukv = w_ukv.reshape(n_l, MLA_KV_RANK, MLA_HEADS, MLA_NOPE + MLA_V)
    wk = wukv[:, :, :, :MLA_NOPE].reshape(n_l, MLA_KV_RANK, MLA_HEADS * MLA_NOPE).astype(BF16)
    wv = wukv[:, :, :, MLA_NOPE:].reshape(n_l, MLA_KV_RANK, MLA_HEADS * MLA_V).astype(BF16)
    return dict(w_main=w_main, w_kv=w_kv, wuq=wuq, wk=wk, wv=wv,
                w_branch=w_branch.astype(BF16), w_ff2=w_ff2.astype(BF16))


def _kv_of(h, wts, lw, lp, tables):
    p_kv = matmul(h, wts["w_kv"], lw["l"], tn=PK_COLS, name="in_proj_kv")
    return kvprep(p_kv, lp["k_norm_g"], lp["mla_kv_norm_g"], lw["wk"], lw["wv"], tables)


def _mix(h, wts, lw, lp, tables, ctx_kv, x_kv):
    p = matmul(h, wts["w_main"], lw["l"], tm=2048, tn=P_TILE, name="in_proj")
    qbt, qct = qprep(p, lp["q_norm_g"], lp["mla_q_norm_g"], lw["wuq"], tables)
    gqa = lambda kv: None if kv is None else (kv[0], kv[2])
    mla = lambda kv: None if kv is None else (kv[1], kv[3])
    y_b = attention(qbt, gqa(ctx_kv), gqa(x_kv), n_heads=GQA_HEADS, group=GQA_GROUP, dq=GQA_HEAD_DIM,
                    dv=GQA_HEAD_DIM, name="attn_gqa")
    y_c = attention(qct, mla(ctx_kv), mla(x_kv), n_heads=MLA_HEADS, group=1, dq=MLA_QK_PAD, dv=MLA_V,
                    name="attn_mla")
    y_a = short_conv(p, lp["conv_w"])
    y_d = sgu(p, lp["sgu_ln_g"], lp["sgu_ln_b"], lw["sgu_w_s"], lp["sgu_b_s"])
    return merge([y_a, y_b, y_c, y_d], p, lp["b_gate"], lw["w_branch"], lw["l"])


def kernel(x, c, ctx, c_ctx, w_ada, b_ada, norm_mix_g, w_in, b_gate, conv_w, q_norm_g, k_norm_g,
           mla_q_norm_g, mla_kv_norm_g, w_uq, w_ukv, sgu_ln_g, sgu_ln_b, sgu_w_s, sgu_b_s,
           w_branch, w_out, norm_ffn_g, w_ff1, w_ff2, final_norm_g):
    assert x.shape[0] == 1 and c.shape[0] == 1 and ctx.shape[0] == 1
    xs, zs = x[0], ctx[0]
    tables = _rope_tables(xs.shape[0])
    cond_t = jnp.stack([c[0], c_ctx], axis=1)
    X, Z = 0, 1

    wts = _prep_weights(w_in, w_uq, w_ukv, w_branch, w_ff2)
    b_ada3 = b_ada[:, None, :]

    def ffn(s, mod, row, l):
        h = norm_mod(s, norm_ffn_g[l], mod, row=row, k_shift=3, k_scale=4)
        u = matmul(h, w_ff1, l, epilogue="relu2", name="ffn_up")
        return matmul(u, wts["w_ff2"], l, out_dtype=F32, epilogue="residual", res=s, mod=mod, row=row,
                      k_gate=5, tm=512, tn=512, name="ffn_down")

    def out_proj(merged, s, mod, row, l):
        return matmul(merged, w_out, l, out_dtype=F32, epilogue="residual", res=s, mod=mod, row=row,
                      k_gate=2, tn=512, name="out_proj")

    for l in range(DEPTH):
        last = l == DEPTH - 1
        lw = dict(l=l, wuq=wts["wuq"][l], wk=wts["wk"][l], wv=wts["wv"][l], w_branch=wts["w_branch"],
                  sgu_w_s=sgu_w_s[l].astype(BF16))
        lp = dict(b_gate=b_gate[l], conv_w=conv_w[l], q_norm_g=q_norm_g[l], k_norm_g=k_norm_g[l],
                  mla_q_norm_g=mla_q_norm_g[l], mla_kv_norm_g=mla_kv_norm_g[l],
                  sgu_ln_g=sgu_ln_g[l], sgu_ln_b=sgu_ln_b[l], sgu_b_s=sgu_b_s[l])
        mod = adaln(cond_t, w_ada, b_ada3, l)

        hz = norm_mod(zs, norm_mix_g[l], mod, row=Z, k_shift=0, k_scale=1)
        kv_z = _kv_of(hz, wts, lw, lp, None)
        hx = norm_mod(xs, norm_mix_g[l], mod, row=X, k_shift=0, k_scale=1)
        kv_x = _kv_of(hx, wts, lw, lp, tables)
        xs = out_proj(_mix(hx, wts, lw, lp, tables, kv_z, kv_x), xs, mod, X, l)
        xs = ffn(xs, mod, X, l)

        if not last:
            zs = out_proj(_mix(hz, wts, lw, lp, None, kv_z, None), zs, mod, Z, l)
            zs = ffn(zs, mod, Z, l)

    out = norm_mod(xs, final_norm_g, out_dtype=F32)
    return out[None]
```

```python
import functools
import math

import jax
import jax.numpy as jnp
from jax import lax
from jax.experimental import pallas as pl
from jax.experimental.pallas import tpu as pltpu

F32 = jnp.float32
BF16 = jnp.bfloat16

D_MODEL = 2048
DEPTH = 2
GRID_W = 64
ROPE_BASE = 10000.0
EPS = 1e-6
N_BRANCH = 4
BRANCH_W = 1024
CONV_W = 1024
CONV_K = 3
GQA_HEADS = 8
GQA_KV_HEADS = 2
GQA_GROUP = GQA_HEADS // GQA_KV_HEADS
GQA_HEAD_DIM = 128
GQA_KV_W = GQA_KV_HEADS * GQA_HEAD_DIM
MLA_HEADS = 8
MLA_Q_RANK = 512
MLA_KV_RANK = 512
MLA_NOPE = 128
MLA_ROPE = 64
MLA_V = 128
SGU_W = 1024
SGU_GROUPS = 8
SGU_CHUNK = 128
D_FF = 4 * D_MODEL

KV_COLS = 2 * GQA_KV_W + MLA_KV_RANK + MLA_ROPE
Q_COLS = GQA_HEADS * GQA_HEAD_DIM + MLA_Q_RANK
A_COLS = 3 * CONV_W
D_COLS = 2 * SGU_W
G_COLS = N_BRANCH * D_MODEL

LANES = 128
BF16_SUBLANES = 16
MLA_QK_PAD = 256
ATTN_TK = 512
ATTN_R = 512
ATTN_BLOCKS = 2
ATTN_AHEAD = 3
VMEM_LIMIT_BYTES = 48 * 1024 * 1024

P_TILE = 512
P_QB = 0
P_CQ = P_QB + GQA_HEADS * GQA_HEAD_DIM
P_A = P_CQ + MLA_Q_RANK
P_D = P_A + A_COLS
P_G = P_D + D_COLS
P_COLS = P_G + G_COLS
PK_KB = 0
PK_VB = PK_KB + GQA_KV_W
PK_CKV = PK_VB + GQA_KV_W
PK_KR = PK_CKV + MLA_KV_RANK
PK_COLS = PK_KR + LANES


def _params(*sem):
    return pltpu.CompilerParams(dimension_semantics=sem, vmem_limit_bytes=VMEM_LIMIT_BYTES)


def _adaln_kernel(ct_ref, w_ref, b_ref, o_ref):
    ct = ct_ref[...]
    s = ct / (1.0 + jnp.exp(-ct))
    w = w_ref[...]
    for r in range(2):
        o_ref[r:r + 1, :] = jnp.sum(w * s[:, r:r + 1], axis=0, keepdims=True) + b_ref[...]


def adaln(cond_t, w, b, l, tn=512):
    _, d, n = w.shape
    return pl.pallas_call(
        _adaln_kernel,
        grid=(n // tn,),
        in_specs=[pl.BlockSpec((d, 2), lambda j: (0, 0)),
                  pl.BlockSpec((None, d, tn), lambda j: (l, 0, j)),
                  pl.BlockSpec((None, 1, tn), lambda j: (l, 0, j))],
        out_specs=pl.BlockSpec((2, tn), lambda j: (0, j)),
        out_shape=jax.ShapeDtypeStruct((2, n), F32),
        compiler_params=_params("arbitrary"),
        name="adaln",
    )(cond_t, w, b)


def _norm_kernel(x_ref, g_ref, *rest, row):
    o_ref = rest[-1]
    gain = g_ref[...]
    if len(rest) == 3:
        sh_ref, sc_ref = rest[0], rest[1]
        gain = gain * (1.0 + sc_ref[row:row + 1, :])
        shift = sh_ref[row:row + 1, :]

    def strip(r, carry):
        rows = pl.ds(pl.multiple_of(r * BF16_SUBLANES, BF16_SUBLANES), BF16_SUBLANES)
        x = x_ref[rows, :]
        y = x * lax.rsqrt(jnp.mean(x * x, axis=-1, keepdims=True) + EPS) * gain
        if len(rest) == 3:
            y = y + shift
        o_ref[rows, :] = y.astype(o_ref.dtype)
        return carry

    lax.fori_loop(0, x_ref.shape[0] // BF16_SUBLANES, strip, 0, unroll=4)


def norm_mod(x, g, mod=None, row=0, k_shift=0, k_scale=1, out_dtype=BF16, tm=512):
    m, d = x.shape
    tm = min(tm, m)
    in_specs = [pl.BlockSpec((tm, d), lambda i: (i, 0)), pl.BlockSpec((1, d), lambda i: (0, 0))]
    args = [x, g.reshape(1, d)]
    if mod is not None:
        in_specs += [pl.BlockSpec((2, d), lambda i: (0, k_shift)), pl.BlockSpec((2, d), lambda i: (0, k_scale))]
        args += [mod, mod]
    return pl.pallas_call(
        functools.partial(_norm_kernel, row=row),
        grid=(m // tm,),
        in_specs=in_specs,
        out_specs=pl.BlockSpec((tm, d), lambda i: (i, 0)),
        out_shape=jax.ShapeDtypeStruct((m, d), out_dtype),
        compiler_params=_params("parallel"),
        name="norm_mod",
    )(*args)


def _mm_kernel(a_ref, w_ref, *rest, epilogue, row, cast_w):
    if cast_w:
        wb_ref = rest[-1]
        rest = rest[:-1]

        @pl.when(pl.program_id(1) == 0)
        def _():
            wb_ref[...] = w_ref[...].astype(BF16)

        w = wb_ref[...]
    else:
        w = w_ref[...]
    o_ref = rest[-1]
    acc = jnp.dot(a_ref[...], w, preferred_element_type=F32)
    if epilogue == "relu2":
        r = jnp.maximum(acc, 0.0)
        acc = r * r
    elif epilogue == "residual":
        res_ref, gate_ref = rest[0], rest[1]
        acc = res_ref[...] + gate_ref[row:row + 1, :] * acc
    o_ref[...] = acc.astype(o_ref.dtype)


def matmul(a, w, l, *, out_dtype=BF16, epilogue="none", res=None, mod=None, row=0, k_gate=0,
           tm=1024, tn=1024, name="matmul"):
    m, k = a.shape
    n = w.shape[2]
    tm, tn = min(tm, m), min(tn, n)
    assert m % tm == 0 and n % tn == 0
    cast_w = w.dtype == F32
    order = (lambda f: (lambda j, i: f(i, j))) if cast_w else (lambda f: f)
    in_specs = [pl.BlockSpec((tm, k), order(lambda i, j: (i, 0))),
                pl.BlockSpec((None, k, tn), order(lambda i, j: (l, 0, j)))]
    args = [a, w]
    if epilogue == "residual":
        kb = k_gate * (n // tn)
        in_specs += [pl.BlockSpec((tm, tn), order(lambda i, j: (i, j))),
                     pl.BlockSpec((2, tn), order(lambda i, j: (0, kb + j)))]
        args += [res, mod]
    grid = (n // tn, m // tm) if cast_w else (m // tm, n // tn)
    return pl.pallas_call(
        functools.partial(_mm_kernel, epilogue=epilogue, row=row, cast_w=cast_w),
        grid=grid,
        in_specs=in_specs,
        out_specs=pl.BlockSpec((tm, tn), order(lambda i, j: (i, j))),
        out_shape=jax.ShapeDtypeStruct((m, n), out_dtype),
        scratch_shapes=[pltpu.VMEM((k, tn), BF16)] if cast_w else [],
        compiler_params=_params("parallel", "arbitrary"),
        name=name,
    )(*args)


def _rope(y, cos, sin, seg):
    lane = lax.broadcasted_iota(jnp.int32, y.shape, 1)
    first = (lane // seg) % 2 == 0
    partner = jnp.where(first, pltpu.roll(y, LANES - seg, 1), pltpu.roll(y, seg, 1))
    return y * cos + partner * sin


def _transpose_tiles(x):
    i0 = lax.broadcasted_iota(jnp.int32, (LANES, LANES), 0)
    i1 = lax.broadcasted_iota(jnp.int32, (LANES, LANES), 1)
    eye = jnp.where(i0 == i1, 1.0, 0.0).astype(x.dtype)
    tiles = [lax.dot_general(eye, x[:, k:k + LANES], (((1,), (1,)), ((), ())), preferred_element_type=F32)
             for k in range(0, x.shape[1], LANES)]
    return jnp.concatenate(tiles, axis=0).astype(x.dtype)


def _head_rms(x, g):
    return x * lax.rsqrt(jnp.mean(x * x, axis=-1, keepdims=True) + EPS) * g


LOG2_E = math.log2(math.e)
GQA_SEG = GQA_HEAD_DIM // 4
MLA_SEG = MLA_ROPE // 4


def _qprep_kernel(qb_ref, cq_ref, qg_ref, cqg_ref, wuq_ref, *rest, rope):
    if rope:
        cb_ref, sb_ref, cc_ref, sc_ref, oqb_ref, oqc_ref = rest
    else:
        oqb_ref, oqc_ref = rest
    qg = qg_ref[...]
    scale_b = LOG2_E * GQA_HEAD_DIM ** -0.5
    for h in range(GQA_HEADS):
        sl = slice(h * GQA_HEAD_DIM, (h + 1) * GQA_HEAD_DIM)
        y = _head_rms(qb_ref[:, sl].astype(F32), qg)
        if rope:
            y = _rope(y, cb_ref[...], sb_ref[...], GQA_SEG)
        oqb_ref[sl, :] = _transpose_tiles((y * scale_b).astype(oqb_ref.dtype))
    cq = _head_rms(cq_ref[...].astype(F32), cqg_ref[...]).astype(BF16)
    qc = jnp.dot(cq, wuq_ref[...], preferred_element_type=F32)
    scale_c = LOG2_E * (MLA_NOPE + MLA_ROPE) ** -0.5
    for h in range(MLA_HEADS):
        nope = slice(h * MLA_QK_PAD, h * MLA_QK_PAD + MLA_NOPE)
        rot = slice(h * MLA_QK_PAD + MLA_NOPE, (h + 1) * MLA_QK_PAD)
        oqc_ref[nope, :] = _transpose_tiles((qc[:, nope] * scale_c).astype(oqc_ref.dtype))
        y = qc[:, rot]
        if rope:
            y = _rope(y, cc_ref[...], sc_ref[...], MLA_SEG)
        oqc_ref[rot, :] = _transpose_tiles((y * scale_c).astype(oqc_ref.dtype))


def qprep(p, qg, cqg, wuq, tables, tm=256):
    m = p.shape[0]
    tm = min(tm, m)
    wq = GQA_HEADS * GQA_HEAD_DIM
    qb_blk = P_QB // wq
    cq_blk = P_CQ // MLA_Q_RANK
    assert P_QB % wq == 0 and P_CQ % MLA_Q_RANK == 0
    in_specs = [pl.BlockSpec((tm, wq), lambda i: (i, qb_blk)),
                pl.BlockSpec((tm, MLA_Q_RANK), lambda i: (i, cq_blk)),
                pl.BlockSpec((1, GQA_HEAD_DIM), lambda i: (0, 0)),
                pl.BlockSpec((1, MLA_Q_RANK), lambda i: (0, 0)),
                pl.BlockSpec(wuq.shape, lambda i: (0, 0))]
    args = [p, p, qg.reshape(1, -1), cqg.reshape(1, -1), wuq]
    if tables is not None:
        in_specs += [pl.BlockSpec((tm, LANES), lambda i: (i, 0))] * 4
        args += list(tables)
    wc = MLA_HEADS * MLA_QK_PAD
    return pl.pallas_call(
        functools.partial(_qprep_kernel, rope=tables is not None),
        grid=(m // tm,),
        in_specs=in_specs,
        out_specs=[pl.BlockSpec((wq, tm), lambda i: (0, i)), pl.BlockSpec((wc, tm), lambda i: (0, i))],
        out_shape=[jax.ShapeDtypeStruct((wq, m), BF16), jax.ShapeDtypeStruct((wc, m), BF16)],
        compiler_params=_params("parallel"),
        name="qprep",
    )(*args)


def _kvprep_kernel(kb_ref, vb_ref, ckv_ref, kr_ref, kg_ref, ckvg_ref, wk_ref, wv_ref, *rest, rope):
    if rope:
        cb_ref, sb_ref, cc_ref, sc_ref, okb_ref, okc_ref, ovb_ref, ovc_ref = rest
    else:
        okb_ref, okc_ref, ovb_ref, ovc_ref = rest
    kg = kg_ref[...]
    for h in range(GQA_KV_HEADS):
        sl = slice(h * GQA_HEAD_DIM, (h + 1) * GQA_HEAD_DIM)
        y = _head_rms(kb_ref[:, sl].astype(F32), kg)
        if rope:
            y = _rope(y, cb_ref[...], sb_ref[...], GQA_SEG)
        okb_ref[:, sl] = y.astype(okb_ref.dtype)
    ovb_ref[...] = _transpose_tiles(vb_ref[...])
    ckv = _head_rms(ckv_ref[...].astype(F32), ckvg_ref[...]).astype(BF16)
    knope = jnp.dot(ckv, wk_ref[...], preferred_element_type=F32)
    ovc_ref[...] = _transpose_tiles(jnp.dot(ckv, wv_ref[...], preferred_element_type=F32).astype(ovc_ref.dtype))
    kr = kr_ref[...].astype(F32)
    if rope:
        kr = _rope(kr, cc_ref[...], sc_ref[...], MLA_SEG)
    kr = kr.astype(okc_ref.dtype)
    for h in range(MLA_HEADS):
        okc_ref[:, h * MLA_QK_PAD:h * MLA_QK_PAD + MLA_NOPE] = (
            knope[:, h * MLA_NOPE:(h + 1) * MLA_NOPE].astype(okc_ref.dtype))
        okc_ref[:, h * MLA_QK_PAD + MLA_NOPE:(h + 1) * MLA_QK_PAD] = kr


def kvprep(p, kg, ckvg, wk, wv, tables):
    m = p.shape[0]
    tm = min(ATTN_TK, m)
    kb_blk = PK_KB // GQA_KV_W
    vb_blk = PK_VB // GQA_KV_W
    ckv_blk = PK_CKV // MLA_KV_RANK
    kr_blk = PK_KR // LANES
    assert PK_KB % GQA_KV_W == 0 and PK_VB % GQA_KV_W == 0 and PK_CKV % MLA_KV_RANK == 0 and PK_KR % LANES == 0
    in_specs = [pl.BlockSpec((tm, GQA_KV_W), lambda i: (i, kb_blk)),
                pl.BlockSpec((tm, GQA_KV_W), lambda i: (i, vb_blk)),
                pl.BlockSpec((tm, MLA_KV_RANK), lambda i: (i, ckv_blk)),
                pl.BlockSpec((tm, LANES), lambda i: (i, kr_blk)),
                pl.BlockSpec((1, GQA_HEAD_DIM), lambda i: (0, 0)),
                pl.BlockSpec((1, MLA_KV_RANK), lambda i: (0, 0)),
                pl.BlockSpec(wk.shape, lambda i: (0, 0)),
                pl.BlockSpec(wv.shape, lambda i: (0, 0))]
    args = [p, p, p, p, kg.reshape(1, -1), ckvg.reshape(1, -1), wk, wv]
    if tables is not None:
        in_specs += [pl.BlockSpec((tm, LANES), lambda i: (i, 0))] * 4
        args += list(tables)
    wkc = MLA_HEADS * MLA_QK_PAD
    wvc = MLA_HEADS * MLA_V
    n = m // tm
    return pl.pallas_call(
        functools.partial(_kvprep_kernel, rope=tables is not None),
        grid=(n,),
        in_specs=in_specs,
        out_specs=[pl.BlockSpec((tm, GQA_KV_W), lambda i: (i, 0)),
                   pl.BlockSpec((tm, wkc), lambda i: (i, 0)),
                   pl.BlockSpec((None, GQA_KV_W, tm), lambda i: (i, 0, 0)),
                   pl.BlockSpec((None, wvc, tm), lambda i: (i, 0, 0))],
        out_shape=[jax.ShapeDtypeStruct((m, GQA_KV_W), BF16),
                   jax.ShapeDtypeStruct((m, wkc), BF16),
                   jax.ShapeDtypeStruct((n, GQA_KV_W, tm), BF16),
                   jax.ShapeDtypeStruct((n, wvc, tm), BF16)],
        compiler_params=_params("parallel"),
        name="kvprep",
    )(*args)


def _attn_kernel(*refs, has_x):
    if has_x:
        qt_ref, kc_ref, vct_ref, kx_ref, vxt_ref, o_ref, s_ref = refs
        n, _, tk = vxt_ref.shape
    else:
        qt_ref, kc_ref, vct_ref, o_ref, s_ref = refs
        n, tk = 0, 0
    n_slot, _, r = s_ref.shape
    rows = [kc_ref.shape[0]] + [tk] * n
    items = [(b, g) for b in range(qt_ref.shape[1] // r) for g in range(n + 1)]

    def keys(g):
        return kc_ref[...] if g == 0 else kx_ref[(g - 1) * tk:g * tk, :]

    def values_t(g):
        return vct_ref[...] if g == 0 else vxt_ref[g - 1]

    def qk(t):
        b, g = items[t]
        st = jnp.dot(keys(g), qt_ref[:, b * r:(b + 1) * r], preferred_element_type=F32)
        s_ref[t % n_slot, :rows[g], :] = st
        return jnp.max(st, axis=0, keepdims=True)

    chunk_max = {t: qk(t) for t in range(min(ATTN_AHEAD, len(items)))}
    m = l = acc = None
    for t, (b, g) in enumerate(items):
        if t + ATTN_AHEAD < len(items):
            chunk_max[t + ATTN_AHEAD] = qk(t + ATTN_AHEAD)
        m_new = chunk_max.pop(t) if g == 0 else jnp.maximum(m, chunk_max.pop(t))
        pt = jnp.exp2(s_ref[t % n_slot, :rows[g], :] - m_new)
        p_sum = jnp.sum(pt, axis=0, keepdims=True)
        pv = jnp.dot(values_t(g), pt.astype(BF16), preferred_element_type=F32)
        if g == 0:
            l, acc = p_sum, pv
        else:
            alpha = jnp.exp2(m - m_new)
            l, acc = alpha * l + p_sum, alpha * acc + pv
        m = m_new
        if g == n:
            o_ref[b * r:(b + 1) * r, :] = (acc / l).T.astype(o_ref.dtype)


def attention(qt, ctx_kv, x_kv, *, n_heads, group, dq, dv, name="attn"):
    s_len = qt.shape[1]
    r = min(ATTN_R, s_len)
    tq = min(ATTN_BLOCKS * r, s_len)
    kc, vct = ctx_kv
    tc = kc.shape[0]
    in_specs = [pl.BlockSpec((dq, tq), lambda h, i: (h, i)),
                pl.BlockSpec((tc, dq), lambda h, i: (0, h // group)),
                pl.BlockSpec((None, dv, tc), lambda h, i: (0, h // group, 0))]
    args = [qt, kc, vct]
    n_chunk, rows = 1, tc
    if x_kv is not None:
        kx, vxt = x_kv
        in_specs += [pl.BlockSpec((kx.shape[0], dq), lambda h, i: (0, h // group)),
                     pl.BlockSpec((vxt.shape[0], dv, vxt.shape[2]), lambda h, i: (0, h // group, 0))]
        args += [kx, vxt]
        n_chunk, rows = 1 + vxt.shape[0], max(tc, vxt.shape[2])
    return pl.pallas_call(
        functools.partial(_attn_kernel, has_x=x_kv is not None),
        grid=(n_heads, s_len // tq),
        in_specs=in_specs,
        out_specs=pl.BlockSpec((tq, dv), lambda h, i: (i, h)),
        out_shape=jax.ShapeDtypeStruct((s_len, n_heads * dv), BF16),
        scratch_shapes=[pltpu.VMEM((min(ATTN_AHEAD + 1, n_chunk), rows, r), F32)],
        compiler_params=_params("parallel", "arbitrary"),
        name=name,
    )(*args)


def _conv_kernel(gb_ref, gc_ref, xa_ref, gcp_ref, xap_ref, gcn_ref, xan_ref, w_ref, o_ref):
    i = pl.program_id(0)
    tm = gb_ref.shape[0]
    u = gc_ref[...].astype(F32) * xa_ref[...].astype(F32)
    halo = gcp_ref.shape[0]
    u_prev = gcp_ref[halo - 1:halo, :].astype(F32) * xap_ref[halo - 1:halo, :].astype(F32)
    u_next = gcn_ref[0:1, :].astype(F32) * xan_ref[0:1, :].astype(F32)
    u_prev = jnp.where(i == 0, 0.0, u_prev)
    u_next = jnp.where(i == pl.num_programs(0) - 1, 0.0, u_next)
    row = lax.broadcasted_iota(jnp.int32, u.shape, 0)
    below = jnp.where(row == 0, u_prev, pltpu.roll(u, 1, 0))
    above = jnp.where(row == tm - 1, u_next, pltpu.roll(u, tm - 1, 0))
    w = w_ref[...]
    y = below * w[0:1, :] + u * w[1:2, :] + above * w[2:3, :]
    o_ref[...] = (gb_ref[...].astype(F32) * y).astype(o_ref.dtype)


def short_conv(p, w, tm=1024):
    m = p.shape[0]
    tm = min(tm, m)
    c = P_TILE
    nc = CONV_W // c
    blk = P_A // c
    assert P_A % c == 0 and CONV_W % c == 0
    hb = tm // BF16_SUBLANES
    n_hb = m // BF16_SUBLANES
    main = lambda k: pl.BlockSpec((tm, c), lambda i, j: (i, blk + k * nc + j))
    prev = lambda k: pl.BlockSpec((BF16_SUBLANES, c),
                                  lambda i, j: (jnp.maximum(i * hb - 1, 0), blk + k * nc + j))
    nxt = lambda k: pl.BlockSpec((BF16_SUBLANES, c),
                                 lambda i, j: (jnp.minimum((i + 1) * hb, n_hb - 1), blk + k * nc + j))
    return pl.pallas_call(
        _conv_kernel,
        grid=(m // tm, nc),
        in_specs=[main(0), main(1), main(2), prev(1), prev(2), nxt(1), nxt(2),
                  pl.BlockSpec((CONV_K, c), lambda i, j: (0, j))],
        out_specs=pl.BlockSpec((tm, c), lambda i, j: (i, j)),
        out_shape=jax.ShapeDtypeStruct((m, CONV_W), BF16),
        compiler_params=_params("parallel", "parallel"),
        name="short_conv",
    )(p, p, p, p, p, p, p, w)


def _gelu_tanh(x):
    c = math.sqrt(2.0 / math.pi)
    return x * (0.5 * (1.0 + jnp.tanh(c * (x + 0.044715 * (x * x * x)))))


def _sgu_kernel(*refs):
    nb = SGU_W // P_TILE
    u_refs, v_refs = refs[:nb], refs[nb:2 * nb]
    g_ref, b_ref, ws_ref, bs_ref, o_ref = refs[2 * nb:]
    tm = o_ref.shape[0]
    n_chunk = tm // SGU_CHUNK
    gw = SGU_W // SGU_GROUPS
    v = _gelu_tanh(jnp.concatenate([r[...] for r in v_refs], axis=1).astype(F32))
    mu = jnp.mean(v, axis=-1, keepdims=True)
    vc = v - mu
    vn = vc * lax.rsqrt(jnp.mean(vc * vc, axis=-1, keepdims=True) + EPS) * g_ref[...] + b_ref[...]
    vn = vn.astype(BF16)
    for g in range(SGU_GROUPS):
        cols = slice(g * gw, (g + 1) * gw)
        rhs = jnp.concatenate([vn[c * SGU_CHUNK:(c + 1) * SGU_CHUNK, cols] for c in range(n_chunk)], axis=1)
        mixed = jnp.dot(ws_ref[g], rhs, preferred_element_type=F32)
        for c in range(n_chunk):
            rows = slice(c * SGU_CHUNK, (c + 1) * SGU_CHUNK)
            u_ref, u0 = u_refs[g * gw // P_TILE], g * gw % P_TILE
            u = _gelu_tanh(u_ref[rows, u0:u0 + gw].astype(F32))
            o_ref[rows, cols] = (u * (mixed[:, c * gw:(c + 1) * gw] + bs_ref[g])).astype(o_ref.dtype)


def sgu(p, ln_g, ln_b, w_s, b_s, tm=512):
    m = p.shape[0]
    tm = min(tm, m)
    c = SGU_W
    nb = c // P_TILE
    blk = P_D // P_TILE
    assert P_D % P_TILE == 0 and c % P_TILE == 0
    gw = SGU_W // SGU_GROUPS
    bs = jnp.broadcast_to(b_s[:, :, None], (SGU_GROUPS, SGU_CHUNK, gw))
    uv_specs = [pl.BlockSpec((tm, P_TILE), lambda i, k=k: (i, blk + k)) for k in range(2 * nb)]
    return pl.pallas_call(
        _sgu_kernel,
        grid=(m // tm,),
        in_specs=uv_specs + [
                  pl.BlockSpec((1, c), lambda i: (0, 0)),
                  pl.BlockSpec((1, c), lambda i: (0, 0)),
                  pl.BlockSpec(w_s.shape, lambda i: (0, 0, 0)),
                  pl.BlockSpec(bs.shape, lambda i: (0, 0, 0))],
        out_specs=pl.BlockSpec((tm, c), lambda i: (i, 0)),
        out_shape=jax.ShapeDtypeStruct((m, c), BF16),
        compiler_params=_params("parallel"),
        name="sgu",
    )(*([p] * (2 * nb)), ln_g.reshape(1, c), ln_b.reshape(1, c), w_s, bs)


def _merge_kernel(*refs):
    ys, pgs, bgs, wbs, o_ref = refs[0:4], refs[4:8], refs[8:12], refs[12:16], refs[16]
    acc = None
    for i in range(N_BRANCH):
        z = pgs[i][...].astype(F32) + bgs[i][...]
        gate = 1.0 / (1.0 + jnp.exp(-z))
        term = gate * jnp.dot(ys[i][...], wbs[i][...], preferred_element_type=F32)
        acc = term if acc is None else acc + term
    o_ref[...] = acc.astype(o_ref.dtype)


def merge(ys, p, b_gate, w_branch, l, tm=1024, tn=P_TILE):
    m = p.shape[0]
    tm = min(tm, m)
    nj = D_MODEL // tn
    g_blk = P_G // tn
    assert P_G % tn == 0
    in_specs = [pl.BlockSpec((tm, BRANCH_W), lambda i, j: (i, 0))] * N_BRANCH
    in_specs += [pl.BlockSpec((tm, tn), lambda i, j, b=b: (i, g_blk + b * nj + j)) for b in range(N_BRANCH)]
    in_specs += [pl.BlockSpec((1, tn), lambda i, j, b=b: (0, b * nj + j)) for b in range(N_BRANCH)]
    in_specs += [pl.BlockSpec((None, None, BRANCH_W, tn), lambda i, j, b=b: (l, b, 0, j))
                 for b in range(N_BRANCH)]
    return pl.pallas_call(
        _merge_kernel,
        grid=(m // tm, nj),
        in_specs=in_specs,
        out_specs=pl.BlockSpec((tm, tn), lambda i, j: (i, j)),
        out_shape=jax.ShapeDtypeStruct((m, D_MODEL), BF16),
        compiler_params=_params("parallel", "arbitrary"),
        name="merge",
    )(*ys, p, p, p, p, *([b_gate.reshape(1, -1)] * N_BRANCH), *([w_branch] * N_BRANCH))


def _rope_tables(n_tok):
    n_rows = n_tok // GRID_W
    row = jnp.arange(n_rows, dtype=F32)[:, None]
    col = jnp.arange(GRID_W, dtype=F32)[:, None]

    def per_token(rows_tab, cols_tab):
        w = rows_tab.shape[1]
        r = jnp.broadcast_to(rows_tab[:, None, :], (n_rows, GRID_W, w)).reshape(n_tok, w)
        c = jnp.broadcast_to(cols_tab[None, :, :], (n_rows, GRID_W, w)).reshape(n_tok, w)
        return r, c

    def table(rot_dim, pad):
        axis_dim = rot_dim // 2
        inv = ROPE_BASE ** (-jnp.arange(0, axis_dim, 2, dtype=F32) / axis_dim)
        ar, ac = row * inv, col * inv
        cos_r, cos_c = per_token(jnp.cos(ar), jnp.cos(ac))
        sin_r, sin_c = per_token(jnp.sin(ar), jnp.sin(ac))
        cos = jnp.concatenate([cos_r, cos_r, cos_c, cos_c], axis=1)
        sin = jnp.concatenate([-sin_r, sin_r, -sin_c, sin_c], axis=1)
        if pad:
            cos = jnp.pad(cos, ((0, 0), (0, pad)))
            sin = jnp.pad(sin, ((0, 0), (0, pad)))
        return cos, sin

    cb, sb = table(GQA_HEAD_DIM, 0)
    cc, sc = table(MLA_ROPE, LANES - MLA_ROPE)
    return cb, sb, cc, sc


def _prep_weights(w_in, w_uq, w_ukv, w_branch, w_ff2):
    n_l = w_in.shape[0]
    w_main = w_in[:, :, KV_COLS:].astype(BF16)
    w_kv = jnp.pad(w_in[:, :, :KV_COLS], ((0, 0), (0, 0), (0, PK_COLS - KV_COLS)))
    wuq = w_uq.reshape(n_l, MLA_Q_RANK, MLA_HEADS, MLA_NOPE + MLA_ROPE)
    wuq = jnp.pad(wuq, ((0, 0), (0, 0), (0, 0), (0, MLA_QK_PAD - MLA_NOPE - MLA_ROPE)))
    wuq = wuq.reshape(n_l, MLA_Q_RANK, MLA_HEADS * MLA_QK_PAD).astype(BF16)
    wukv = w_ukv.reshape(n_l, MLA_KV_RANK, MLA_HEADS, MLA_NOPE + MLA_V)
    wk = wukv[:, :, :, :MLA_NOPE].reshape(n_l, MLA_KV_RANK, MLA_HEADS * MLA_NOPE).astype(BF16)
    wv = wukv[:, :, :, MLA_NOPE:].reshape(n_l, MLA_KV_RANK, MLA_HEADS * MLA_V).astype(BF16)
    return dict(w_main=w_main, w_kv=w_kv, wuq=wuq, wk=wk, wv=wv,
                w_branch=w_branch.astype(BF16), w_ff2=w_ff2.astype(BF16))


def _kv_of(h, wts, lw, lp, tables):
    p_kv = matmul(h, wts["w_kv"], lw["l"], tn=PK_COLS, name="in_proj_kv")
    return kvprep(p_kv, lp["k_norm_g"], lp["mla_kv_norm_g"], lw["wk"], lw["wv"], tables)


def _mix(h, wts, lw, lp, tables, ctx_kv, x_kv):
    p = matmul(h, wts["w_main"], lw["l"], tm=2048, tn=P_TILE, name="in_proj")
    qbt, qct = qprep(p, lp["q_norm_g"], lp["mla_q_norm_g"], lw["wuq"], tables)
    gqa = lambda kv: None if kv is None else (kv[0], kv[2])
    mla = lambda kv: None if kv is None else (kv[1], kv[3])
    y_b = attention(qbt, gqa(ctx_kv), gqa(x_kv), n_heads=GQA_HEADS, group=GQA_GROUP, dq=GQA_HEAD_DIM,
                    dv=GQA_HEAD_DIM, name="attn_gqa")
    y_c = attention(qct, mla(ctx_kv), mla(x_kv), n_heads=MLA_HEADS, group=1, dq=MLA_QK_PAD, dv=MLA_V,
                    name="attn_mla")
    y_a = short_conv(p, lp["conv_w"])
    y_d = sgu(p, lp["sgu_ln_g"], lp["sgu_ln_b"], lw["sgu_w_s"], lp["sgu_b_s"])
    return merge([y_a, y_b, y_c, y_d], p, lp["b_gate"], lw["w_branch"], lw["l"])


def kernel(x, c, ctx, c_ctx, w_ada, b_ada, norm_mix_g, w_in, b_gate, conv_w, q_norm_g, k_norm_g,
           mla_q_norm_g, mla_kv_norm_g, w_uq, w_ukv, sgu_ln_g, sgu_ln_b, sgu_w_s, sgu_b_s,
           w_branch, w_out, norm_ffn_g, w_ff1, w_ff2, final_norm_g):
    assert x.shape[0] == 1 and c.shape[0] == 1 and ctx.shape[0] == 1
    xs, zs = x[0], ctx[0]
    tables = _rope_tables(xs.shape[0])
    cond_t = jnp.stack([c[0], c_ctx], axis=1)
    X, Z = 0, 1

    wts = _prep_weights(w_in, w_uq, w_ukv, w_branch, w_ff2)
    b_ada3 = b_ada[:, None, :]

    def ffn(s, mod, row, l):
        h = norm_mod(s, norm_ffn_g[l], mod, row=row, k_shift=3, k_scale=4)
        u = matmul(h, w_ff1, l, epilogue="relu2", name="ffn_up")
        return matmul(u, wts["w_ff2"], l, out_dtype=F32, epilogue="residual", res=s, mod=mod, row=row,
                      k_gate=5, tm=1024, tn=256, name="ffn_down")

    def out_proj(merged, s, mod, row, l):
        return matmul(merged, w_out, l, out_dtype=F32, epilogue="residual", res=s, mod=mod, row=row,
                      k_gate=2, tn=512, name="out_proj")

    for l in range(DEPTH):
        last = l == DEPTH - 1
        lw = dict(l=l, wuq=wts["wuq"][l], wk=wts["wk"][l], wv=wts["wv"][l], w_branch=wts["w_branch"],
                  sgu_w_s=sgu_w_s[l].astype(BF16))
        lp = dict(b_gate=b_gate[l], conv_w=conv_w[l], q_norm_g=q_norm_g[l], k_norm_g=k_norm_g[l],
                  mla_q_norm_g=mla_q_norm_g[l], mla_kv_norm_g=mla_kv_norm_g[l],
                  sgu_ln_g=sgu_ln_g[l], sgu_ln_b=sgu_ln_b[l], sgu_b_s=sgu_b_s[l])
        mod = adaln(cond_t, w_ada, b_ada3, l)

        hz = norm_mod(zs, norm_mix_g[l], mod, row=Z, k_shift=0, k_scale=1)
        kv_z = _kv_of(hz, wts, lw, lp, None)
        hx = norm_mod(xs, norm_mix_g[l], mod, row=X, k_shift=0, k_scale=1)
        kv_x = _kv_of(hx, wts, lw, lp, tables)
        xs = out_proj(_mix(hx, wts, lw, lp, tables, kv_z, kv_x), xs, mod, X, l)
        xs = ffn(xs, mod, X, l)

        if not last:
            zs = out_proj(_mix(hz, wts, lw, lp, None, kv_z, None), zs, mod, Z, l)
            zs = ffn(zs, mod, Z, l)

    out = norm_mod(xs, final_norm_g, out_dtype=F32)
    return out[None]
```

```python
import functools
import math

import jax
import jax.numpy as jnp
from jax import lax
from jax.experimental import pallas as pl
from jax.experimental.pallas import tpu as pltpu

F32 = jnp.float32
BF16 = jnp.bfloat16

D_MODEL = 2048
DEPTH = 2
GRID_W = 64
ROPE_BASE = 10000.0
EPS = 1e-6
N_BRANCH = 4
BRANCH_W = 1024
CONV_W = 1024
CONV_K = 3
GQA_HEADS = 8
GQA_KV_HEADS = 2
GQA_GROUP = GQA_HEADS // GQA_KV_HEADS
GQA_HEAD_DIM = 128
GQA_KV_W = GQA_KV_HEADS * GQA_HEAD_DIM
MLA_HEADS = 8
MLA_Q_RANK = 512
MLA_KV_RANK = 512
MLA_NOPE = 128
MLA_ROPE = 64
MLA_V = 128
SGU_W = 1024
SGU_GROUPS = 8
SGU_CHUNK = 128
D_FF = 4 * D_MODEL

KV_COLS = 2 * GQA_KV_W + MLA_KV_RANK + MLA_ROPE
Q_COLS = GQA_HEADS * GQA_HEAD_DIM + MLA_Q_RANK
A_COLS = 3 * CONV_W
D_COLS = 2 * SGU_W
G_COLS = N_BRANCH * D_MODEL

LANES = 128
BF16_SUBLANES = 16
MLA_QK_PAD = 256
ATTN_TK = 512
ATTN_R = 512
ATTN_BLOCKS = 2
ATTN_AHEAD = 3
VMEM_LIMIT_BYTES = 48 * 1024 * 1024

P_TILE = 512
P_QB = 0
P_CQ = P_QB + GQA_HEADS * GQA_HEAD_DIM
P_A = P_CQ + MLA_Q_RANK
P_D = P_A + A_COLS
P_G = P_D + D_COLS
P_COLS = P_G + G_COLS
PK_KB = 0
PK_VB = PK_KB + GQA_KV_W
PK_CKV = PK_VB + GQA_KV_W
PK_KR = PK_CKV + MLA_KV_RANK
PK_COLS = PK_KR + LANES


def _params(*sem):
    return pltpu.CompilerParams(dimension_semantics=sem, vmem_limit_bytes=VMEM_LIMIT_BYTES)


def _adaln_kernel(ct_ref, w_ref, b_ref, o_ref):
    ct = ct_ref[...]
    s = ct / (1.0 + jnp.exp(-ct))
    w = w_ref[...]
    for r in range(2):
        o_ref[r:r + 1, :] = jnp.sum(w * s[:, r:r + 1], axis=0, keepdims=True) + b_ref[...]


def adaln(cond_t, w, b, l, tn=512):
    _, d, n = w.shape
    return pl.pallas_call(
        _adaln_kernel,
        grid=(n // tn,),
        in_specs=[pl.BlockSpec((d, 2), lambda j: (0, 0)),
                  pl.BlockSpec((None, d, tn), lambda j: (l, 0, j)),
                  pl.BlockSpec((None, 1, tn), lambda j: (l, 0, j))],
        out_specs=pl.BlockSpec((2, tn), lambda j: (0, j)),
        out_shape=jax.ShapeDtypeStruct((2, n), F32),
        compiler_params=_params("arbitrary"),
        name="adaln",
    )(cond_t, w, b)


def _norm_kernel(x_ref, g_ref, *rest, row):
    o_ref = rest[-1]
    gain = g_ref[...]
    if len(rest) == 3:
        sh_ref, sc_ref = rest[0], rest[1]
        gain = gain * (1.0 + sc_ref[row:row + 1, :])
        shift = sh_ref[row:row + 1, :]

    def strip(r, carry):
        rows = pl.ds(pl.multiple_of(r * BF16_SUBLANES, BF16_SUBLANES), BF16_SUBLANES)
        x = x_ref[rows, :]
        y = x * lax.rsqrt(jnp.mean(x * x, axis=-1, keepdims=True) + EPS) * gain
        if len(rest) == 3:
            y = y + shift
        o_ref[rows, :] = y.astype(o_ref.dtype)
        return carry

    lax.fori_loop(0, x_ref.shape[0] // BF16_SUBLANES, strip, 0, unroll=4)


def norm_mod(x, g, mod=None, row=0, k_shift=0, k_scale=1, out_dtype=BF16, tm=512):
    m, d = x.shape
    tm = min(tm, m)
    in_specs = [pl.BlockSpec((tm, d), lambda i: (i, 0)), pl.BlockSpec((1, d), lambda i: (0, 0))]
    args = [x, g.reshape(1, d)]
    if mod is not None:
        in_specs += [pl.BlockSpec((2, d), lambda i: (0, k_shift)), pl.BlockSpec((2, d), lambda i: (0, k_scale))]
        args += [mod, mod]
    return pl.pallas_call(
        functools.partial(_norm_kernel, row=row),
        grid=(m // tm,),
        in_specs=in_specs,
        out_specs=pl.BlockSpec((tm, d), lambda i: (i, 0)),
        out_shape=jax.ShapeDtypeStruct((m, d), out_dtype),
        compiler_params=_params("parallel"),
        name="norm_mod",
    )(*args)


def _mm_kernel(a_ref, w_ref, *rest, epilogue, row, cast_w):
    if cast_w:
        wb_ref = rest[-1]
        rest = rest[:-1]

        @pl.when(pl.program_id(1) == 0)
        def _():
            wb_ref[...] = w_ref[...].astype(BF16)

        w = wb_ref[...]
    else:
        w = w_ref[...]
    acc = jnp.dot(a_ref[...], w, preferred_element_type=F32)
    if epilogue == "residual_norm":
        res_ref, gate_ref, g_ref, sh_ref, sc_ref, o_ref, h_ref = rest
        x = res_ref[...] + gate_ref[row:row + 1, :] * acc
        o_ref[...] = x
        gain = g_ref[...] * (1.0 + sc_ref[row:row + 1, :])
        y = x * lax.rsqrt(jnp.mean(x * x, axis=-1, keepdims=True) + EPS) * gain + sh_ref[row:row + 1, :]
        h_ref[...] = y.astype(h_ref.dtype)
        return
    o_ref = rest[-1]
    if epilogue == "relu2":
        r = jnp.maximum(acc, 0.0)
        acc = r * r
    elif epilogue == "residual":
        res_ref, gate_ref = rest[0], rest[1]
        acc = res_ref[...] + gate_ref[row:row + 1, :] * acc
    o_ref[...] = acc.astype(o_ref.dtype)


def matmul(a, w, l, *, out_dtype=BF16, epilogue="none", res=None, mod=None, row=0, k_gate=0,
           norm_g=None, k_shift=0, k_scale=0, tm=1024, tn=1024, name="matmul"):
    m, k = a.shape
    n = w.shape[2]
    tm, tn = min(tm, m), min(tn, n)
    assert m % tm == 0 and n % tn == 0
    cast_w = w.dtype == F32
    fused_norm = epilogue == "residual_norm"
    assert not fused_norm or (tn == n and not cast_w)
    order = (lambda f: (lambda j, i: f(i, j))) if cast_w else (lambda f: f)
    in_specs = [pl.BlockSpec((tm, k), order(lambda i, j: (i, 0))),
                pl.BlockSpec((None, k, tn), order(lambda i, j: (l, 0, j)))]
    args = [a, w]
    if epilogue in ("residual", "residual_norm"):
        kb = k_gate * (n // tn)
        in_specs += [pl.BlockSpec((tm, tn), order(lambda i, j: (i, j))),
                     pl.BlockSpec((2, tn), order(lambda i, j: (0, kb + j)))]
        args += [res, mod]
    out_specs = pl.BlockSpec((tm, tn), order(lambda i, j: (i, j)))
    out_shape = jax.ShapeDtypeStruct((m, n), out_dtype)
    if fused_norm:
        in_specs += [pl.BlockSpec((1, n), lambda i, j: (0, 0)),
                     pl.BlockSpec((2, n), lambda i, j: (0, k_shift)),
                     pl.BlockSpec((2, n), lambda i, j: (0, k_scale))]
        args += [norm_g.reshape(1, n), mod, mod]
        out_specs = [out_specs, out_specs]
        out_shape = [out_shape, jax.ShapeDtypeStruct((m, n), BF16)]
    grid = (n // tn, m // tm) if cast_w else (m // tm, n // tn)
    return pl.pallas_call(
        functools.partial(_mm_kernel, epilogue=epilogue, row=row, cast_w=cast_w),
        grid=grid,
        in_specs=in_specs,
        out_specs=out_specs,
        out_shape=out_shape,
        scratch_shapes=[pltpu.VMEM((k, tn), BF16)] if cast_w else [],
        compiler_params=_params("parallel", "arbitrary"),
        name=name,
    )(*args)


def _rope(y, cos, sin, seg):
    lane = lax.broadcasted_iota(jnp.int32, y.shape, 1)
    first = (lane // seg) % 2 == 0
    partner = jnp.where(first, pltpu.roll(y, LANES - seg, 1), pltpu.roll(y, seg, 1))
    return y * cos + partner * sin


def _transpose_tiles(x):
    i0 = lax.broadcasted_iota(jnp.int32, (LANES, LANES), 0)
    i1 = lax.broadcasted_iota(jnp.int32, (LANES, LANES), 1)
    eye = jnp.where(i0 == i1, 1.0, 0.0).astype(x.dtype)
    tiles = [lax.dot_general(eye, x[:, k:k + LANES], (((1,), (1,)), ((), ())), preferred_element_type=F32)
             for k in range(0, x.shape[1], LANES)]
    return jnp.concatenate(tiles, axis=0).astype(x.dtype)


def _head_rms(x, g):
    return x * lax.rsqrt(jnp.mean(x * x, axis=-1, keepdims=True) + EPS) * g


LOG2_E = math.log2(math.e)
GQA_SEG = GQA_HEAD_DIM // 4
MLA_SEG = MLA_ROPE // 4


def _qprep_kernel(qb_ref, cq_ref, qg_ref, cqg_ref, wuq_ref, *rest, rope):
    if rope:
        cb_ref, sb_ref, cc_ref, sc_ref, oqb_ref, oqc_ref = rest
    else:
        oqb_ref, oqc_ref = rest
    qg = qg_ref[...]
    scale_b = LOG2_E * GQA_HEAD_DIM ** -0.5
    for h in range(GQA_HEADS):
        sl = slice(h * GQA_HEAD_DIM, (h + 1) * GQA_HEAD_DIM)
        y = _head_rms(qb_ref[:, sl].astype(F32), qg)
        if rope:
            y = _rope(y, cb_ref[...], sb_ref[...], GQA_SEG)
        oqb_ref[sl, :] = _transpose_tiles((y * scale_b).astype(oqb_ref.dtype))
    cq = _head_rms(cq_ref[...].astype(F32), cqg_ref[...]).astype(BF16)
    qc = jnp.dot(cq, wuq_ref[...], preferred_element_type=F32)
    scale_c = LOG2_E * (MLA_NOPE + MLA_ROPE) ** -0.5
    for h in range(MLA_HEADS):
        nope = slice(h * MLA_QK_PAD, h * MLA_QK_PAD + MLA_NOPE)
        rot = slice(h * MLA_QK_PAD + MLA_NOPE, (h + 1) * MLA_QK_PAD)
        oqc_ref[nope, :] = _transpose_tiles((qc[:, nope] * scale_c).astype(oqc_ref.dtype))
        y = qc[:, rot]
        if rope:
            y = _rope(y, cc_ref[...], sc_ref[...], MLA_SEG)
        oqc_ref[rot, :] = _transpose_tiles((y * scale_c).astype(oqc_ref.dtype))


def qprep(p, qg, cqg, wuq, tables, tm=256):
    m = p.shape[0]
    tm = min(tm, m)
    wq = GQA_HEADS * GQA_HEAD_DIM
    qb_blk = P_QB // wq
    cq_blk = P_CQ // MLA_Q_RANK
    assert P_QB % wq == 0 and P_CQ % MLA_Q_RANK == 0
    in_specs = [pl.BlockSpec((tm, wq), lambda i: (i, qb_blk)),
                pl.BlockSpec((tm, MLA_Q_RANK), lambda i: (i, cq_blk)),
                pl.BlockSpec((1, GQA_HEAD_DIM), lambda i: (0, 0)),
                pl.BlockSpec((1, MLA_Q_RANK), lambda i: (0, 0)),
                pl.BlockSpec(wuq.shape, lambda i: (0, 0))]
    args = [p, p, qg.reshape(1, -1), cqg.reshape(1, -1), wuq]
    if tables is not None:
        in_specs += [pl.BlockSpec((tm, LANES), lambda i: (i, 0))] * 4
        args += list(tables)
    wc = MLA_HEADS * MLA_QK_PAD
    return pl.pallas_call(
        functools.partial(_qprep_kernel, rope=tables is not None),
        grid=(m // tm,),
        in_specs=in_specs,
        out_specs=[pl.BlockSpec((wq, tm), lambda i: (0, i)), pl.BlockSpec((wc, tm), lambda i: (0, i))],
        out_shape=[jax.ShapeDtypeStruct((wq, m), BF16), jax.ShapeDtypeStruct((wc, m), BF16)],
        compiler_params=_params("parallel"),
        name="qprep",
    )(*args)


def _kvprep_kernel(kb_ref, vb_ref, ckv_ref, kr_ref, kg_ref, ckvg_ref, wk_ref, wv_ref, *rest, rope):
    if rope:
        cb_ref, sb_ref, cc_ref, sc_ref, okb_ref, okc_ref, ovb_ref, ovc_ref = rest
    else:
        okb_ref, okc_ref, ovb_ref, ovc_ref = rest
    kg = kg_ref[...]
    for h in range(GQA_KV_HEADS):
        sl = slice(h * GQA_HEAD_DIM, (h + 1) * GQA_HEAD_DIM)
        y = _head_rms(kb_ref[:, sl].astype(F32), kg)
        if rope:
            y = _rope(y, cb_ref[...], sb_ref[...], GQA_SEG)
        okb_ref[:, sl] = y.astype(okb_ref.dtype)
    ovb_ref[...] = _transpose_tiles(vb_ref[...])
    ckv = _head_rms(ckv_ref[...].astype(F32), ckvg_ref[...]).astype(BF16)
    knope = jnp.dot(ckv, wk_ref[...], preferred_element_type=F32)
    ovc_ref[...] = _transpose_tiles(jnp.dot(ckv, wv_ref[...], preferred_element_type=F32).astype(ovc_ref.dtype))
    kr = kr_ref[...].astype(F32)
    if rope:
        kr = _rope(kr, cc_ref[...], sc_ref[...], MLA_SEG)
    kr = kr.astype(okc_ref.dtype)
    for h in range(MLA_HEADS):
        okc_ref[:, h * MLA_QK_PAD:h * MLA_QK_PAD + MLA_NOPE] = (
            knope[:, h * MLA_NOPE:(h + 1) * MLA_NOPE].astype(okc_ref.dtype))
        okc_ref[:, h * MLA_QK_PAD + MLA_NOPE:(h + 1) * MLA_QK_PAD] = kr


def kvprep(p, kg, ckvg, wk, wv, tables):
    m = p.shape[0]
    tm = min(ATTN_TK, m)
    kb_blk = PK_KB // GQA_KV_W
    vb_blk = PK_VB // GQA_KV_W
    ckv_blk = PK_CKV // MLA_KV_RANK
    kr_blk = PK_KR // LANES
    assert PK_KB % GQA_KV_W == 0 and PK_VB % GQA_KV_W == 0 and PK_CKV % MLA_KV_RANK == 0 and PK_KR % LANES == 0
    in_specs = [pl.BlockSpec((tm, GQA_KV_W), lambda i: (i, kb_blk)),
                pl.BlockSpec((tm, GQA_KV_W), lambda i: (i, vb_blk)),
                pl.BlockSpec((tm, MLA_KV_RANK), lambda i: (i, ckv_blk)),
                pl.BlockSpec((tm, LANES), lambda i: (i, kr_blk)),
                pl.BlockSpec((1, GQA_HEAD_DIM), lambda i: (0, 0)),
                pl.BlockSpec((1, MLA_KV_RANK), lambda i: (0, 0)),
                pl.BlockSpec(wk.shape, lambda i: (0, 0)),
                pl.BlockSpec(wv.shape, lambda i: (0, 0))]
    args = [p, p, p, p, kg.reshape(1, -1), ckvg.reshape(1, -1), wk, wv]
    if tables is not None:
        in_specs += [pl.BlockSpec((tm, LANES), lambda i: (i, 0))] * 4
        args += list(tables)
    wkc = MLA_HEADS * MLA_QK_PAD
    wvc = MLA_HEADS * MLA_V
    n = m // tm
    return pl.pallas_call(
        functools.partial(_kvprep_kernel, rope=tables is not None),
        grid=(n,),
        in_specs=in_specs,
        out_specs=[pl.BlockSpec((tm, GQA_KV_W), lambda i: (i, 0)),
                   pl.BlockSpec((tm, wkc), lambda i: (i, 0)),
                   pl.BlockSpec((None, GQA_KV_W, tm), lambda i: (i, 0, 0)),
                   pl.BlockSpec((None, wvc, tm), lambda i: (i, 0, 0))],
        out_shape=[jax.ShapeDtypeStruct((m, GQA_KV_W), BF16),
                   jax.ShapeDtypeStruct((m, wkc), BF16),
                   jax.ShapeDtypeStruct((n, GQA_KV_W, tm), BF16),
                   jax.ShapeDtypeStruct((n, wvc, tm), BF16)],
        compiler_params=_params("parallel"),
        name="kvprep",
    )(*args)


def _attn_kernel(*refs, has_x):
    if has_x:
        qt_ref, kc_ref, vct_ref, kx_ref, vxt_ref, o_ref, s_ref = refs
        n, _, tk = vxt_ref.shape
    else:
        qt_ref, kc_ref, vct_ref, o_ref, s_ref = refs
        n, tk = 0, 0
    n_slot, _, r = s_ref.shape
    rows = [kc_ref.shape[0]] + [tk] * n
    items = [(b, g) for b in range(qt_ref.shape[1] // r) for g in range(n + 1)]

    def keys(g):
        return kc_ref[...] if g == 0 else kx_ref[(g - 1) * tk:g * tk, :]

    def values_t(g):
        return vct_ref[...] if g == 0 else vxt_ref[g - 1]

    def qk(t):
        b, g = items[t]
        st = jnp.dot(keys(g), qt_ref[:, b * r:(b + 1) * r], preferred_element_type=F32)
        s_ref[t % n_slot, :rows[g], :] = st
        return jnp.max(st, axis=0, keepdims=True)

    chunk_max = {t: qk(t) for t in range(min(ATTN_AHEAD, len(items)))}
    m = l = acc = None
    for t, (b, g) in enumerate(items):
        if t + ATTN_AHEAD < len(items):
            chunk_max[t + ATTN_AHEAD] = qk(t + ATTN_AHEAD)
        m_new = chunk_max.pop(t) if g == 0 else jnp.maximum(m, chunk_max.pop(t))
        pt = jnp.exp2(s_ref[t % n_slot, :rows[g], :] - m_new)
        p_sum = jnp.sum(pt, axis=0, keepdims=True)
        pv = jnp.dot(values_t(g), pt.astype(BF16), preferred_element_type=F32)
        if g == 0:
            l, acc = p_sum, pv
        else:
            alpha = jnp.exp2(m - m_new)
            l, acc = alpha * l + p_sum, alpha * acc + pv
        m = m_new
        if g == n:
            o_ref[b * r:(b + 1) * r, :] = (acc / l).T.astype(o_ref.dtype)


def attention(qt, ctx_kv, x_kv, *, n_heads, group, dq, dv, name="attn"):
    s_len = qt.shape[1]
    r = min(ATTN_R, s_len)
    tq = min(ATTN_BLOCKS * r, s_len)
    kc, vct = ctx_kv
    tc = kc.shape[0]
    in_specs = [pl.BlockSpec((dq, tq), lambda h, i: (h, i)),
                pl.BlockSpec((tc, dq), lambda h, i: (0, h // group)),
                pl.BlockSpec((None, dv, tc), lambda h, i: (0, h // group, 0))]
    args = [qt, kc, vct]
    n_chunk, rows = 1, tc
    if x_kv is not None:
        kx, vxt = x_kv
        in_specs += [pl.BlockSpec((kx.shape[0], dq), lambda h, i: (0, h // group)),
                     pl.BlockSpec((vxt.shape[0], dv, vxt.shape[2]), lambda h, i: (0, h // group, 0))]
        args += [kx, vxt]
        n_chunk, rows = 1 + vxt.shape[0], max(tc, vxt.shape[2])
    return pl.pallas_call(
        functools.partial(_attn_kernel, has_x=x_kv is not None),
        grid=(n_heads, s_len // tq),
        in_specs=in_specs,
        out_specs=pl.BlockSpec((tq, dv), lambda h, i: (i, h)),
        out_shape=jax.ShapeDtypeStruct((s_len, n_heads * dv), BF16),
        scratch_shapes=[pltpu.VMEM((min(ATTN_AHEAD + 1, n_chunk), rows, r), F32)],
        compiler_params=_params("parallel", "arbitrary"),
        name=name,
    )(*args)


def _conv_kernel(gb_ref, gc_ref, xa_ref, gcp_ref, xap_ref, gcn_ref, xan_ref, w_ref, o_ref):
    i = pl.program_id(0)
    tm = gb_ref.shape[0]
    u = gc_ref[...].astype(F32) * xa_ref[...].astype(F32)
    halo = gcp_ref.shape[0]
    u_prev = gcp_ref[halo - 1:halo, :].astype(F32) * xap_ref[halo - 1:halo, :].astype(F32)
    u_next = gcn_ref[0:1, :].astype(F32) * xan_ref[0:1, :].astype(F32)
    u_prev = jnp.where(i == 0, 0.0, u_prev)
    u_next = jnp.where(i == pl.num_programs(0) - 1, 0.0, u_next)
    row = lax.broadcasted_iota(jnp.int32, u.shape, 0)
    below = jnp.where(row == 0, u_prev, pltpu.roll(u, 1, 0))
    above = jnp.where(row == tm - 1, u_next, pltpu.roll(u, tm - 1, 0))
    w = w_ref[...]
    y = below * w[0:1, :] + u * w[1:2, :] + above * w[2:3, :]
    o_ref[...] = (gb_ref[...].astype(F32) * y).astype(o_ref.dtype)


def short_conv(p, w, tm=1024):
    m = p.shape[0]
    tm = min(tm, m)
    c = P_TILE
    nc = CONV_W // c
    blk = P_A // c
    assert P_A % c == 0 and CONV_W % c == 0
    hb = tm // BF16_SUBLANES
    n_hb = m // BF16_SUBLANES
    main = lambda k: pl.BlockSpec((tm, c), lambda i, j: (i, blk + k * nc + j))
    prev = lambda k: pl.BlockSpec((BF16_SUBLANES, c),
                                  lambda i, j: (jnp.maximum(i * hb - 1, 0), blk + k * nc + j))
    nxt = lambda k: pl.BlockSpec((BF16_SUBLANES, c),
                                 lambda i, j: (jnp.minimum((i + 1) * hb, n_hb - 1), blk + k * nc + j))
    return pl.pallas_call(
        _conv_kernel,
        grid=(m // tm, nc),
        in_specs=[main(0), main(1), main(2), prev(1), prev(2), nxt(1), nxt(2),
                  pl.BlockSpec((CONV_K, c), lambda i, j: (0, j))],
        out_specs=pl.BlockSpec((tm, c), lambda i, j: (i, j)),
        out_shape=jax.ShapeDtypeStruct((m, CONV_W), BF16),
        compiler_params=_params("parallel", "parallel"),
        name="short_conv",
    )(p, p, p, p, p, p, p, w)


def _gelu_tanh(x):
    c = math.sqrt(2.0 / math.pi)
    return x * (0.5 * (1.0 + jnp.tanh(c * (x + 0.044715 * (x * x * x)))))


def _sgu_kernel(*refs):
    nb = SGU_W // P_TILE
    u_refs, v_refs = refs[:nb], refs[nb:2 * nb]
    g_ref, b_ref, ws_ref, bs_ref, o_ref = refs[2 * nb:]
    tm = o_ref.shape[0]
    n_chunk = tm // SGU_CHUNK
    gw = SGU_W // SGU_GROUPS
    v = _gelu_tanh(jnp.concatenate([r[...] for r in v_refs], axis=1).astype(F32))
    mu = jnp.mean(v, axis=-1, keepdims=True)
    vc = v - mu
    vn = vc * lax.rsqrt(jnp.mean(vc * vc, axis=-1, keepdims=True) + EPS) * g_ref[...] + b_ref[...]
    vn = vn.astype(BF16)
    for g in range(SGU_GROUPS):
        cols = slice(g * gw, (g + 1) * gw)
        rhs = jnp.concatenate([vn[c * SGU_CHUNK:(c + 1) * SGU_CHUNK, cols] for c in range(n_chunk)], axis=1)
        mixed = jnp.dot(ws_ref[g], rhs, preferred_element_type=F32)
        for c in range(n_chunk):
            rows = slice(c * SGU_CHUNK, (c + 1) * SGU_CHUNK)
            u_ref, u0 = u_refs[g * gw // P_TILE], g * gw % P_TILE
            u = _gelu_tanh(u_ref[rows, u0:u0 + gw].astype(F32))
            o_ref[rows, cols] = (u * (mixed[:, c * gw:(c + 1) * gw] + bs_ref[g])).astype(o_ref.dtype)


def sgu(p, ln_g, ln_b, w_s, b_s, tm=512):
    m = p.shape[0]
    tm = min(tm, m)
    c = SGU_W
    nb = c // P_TILE
    blk = P_D // P_TILE
    assert P_D % P_TILE == 0 and c % P_TILE == 0
    gw = SGU_W // SGU_GROUPS
    bs = jnp.broadcast_to(b_s[:, :, None], (SGU_GROUPS, SGU_CHUNK, gw))
    uv_specs = [pl.BlockSpec((tm, P_TILE), lambda i, k=k: (i, blk + k)) for k in range(2 * nb)]
    return pl.pallas_call(
        _sgu_kernel,
        grid=(m // tm,),
        in_specs=uv_specs + [
                  pl.BlockSpec((1, c), lambda i: (0, 0)),
                  pl.BlockSpec((1, c), lambda i: (0, 0)),
                  pl.BlockSpec(w_s.shape, lambda i: (0, 0, 0)),
                  pl.BlockSpec(bs.shape, lambda i: (0, 0, 0))],
        out_specs=pl.BlockSpec((tm, c), lambda i: (i, 0)),
        out_shape=jax.ShapeDtypeStruct((m, c), BF16),
        compiler_params=_params("parallel"),
        name="sgu",
    )(*([p] * (2 * nb)), ln_g.reshape(1, c), ln_b.reshape(1, c), w_s, bs)


def _merge_kernel(*refs):
    ys, pgs, bgs, wbs, o_ref = refs[0:4], refs[4:8], refs[8:12], refs[12:16], refs[16]
    acc = None
    for i in range(N_BRANCH):
        z = pgs[i][...].astype(F32) + bgs[i][...]
        gate = 1.0 / (1.0 + jnp.exp(-z))
        term = gate * jnp.dot(ys[i][...], wbs[i][...], preferred_element_type=F32)
        acc = term if acc is None else acc + term
    o_ref[...] = acc.astype(o_ref.dtype)


def merge(ys, p, b_gate, w_branch, l, tm=1024, tn=P_TILE):
    m = p.shape[0]
    tm = min(tm, m)
    nj = D_MODEL // tn
    g_blk = P_G // tn
    assert P_G % tn == 0
    in_specs = [pl.BlockSpec((tm, BRANCH_W), lambda i, j: (i, 0))] * N_BRANCH
    in_specs += [pl.BlockSpec((tm, tn), lambda i, j, b=b: (i, g_blk + b * nj + j)) for b in range(N_BRANCH)]
    in_specs += [pl.BlockSpec((1, tn), lambda i, j, b=b: (0, b * nj + j)) for b in range(N_BRANCH)]
    in_specs += [pl.BlockSpec((None, None, BRANCH_W, tn), lambda i, j, b=b: (l, b, 0, j))
                 for b in range(N_BRANCH)]
    return pl.pallas_call(
        _merge_kernel,
        grid=(m // tm, nj),
        in_specs=in_specs,
        out_specs=pl.BlockSpec((tm, tn), lambda i, j: (i, j)),
        out_shape=jax.ShapeDtypeStruct((m, D_MODEL), BF16),
        compiler_params=_params("parallel", "arbitrary"),
        name="merge",
    )(*ys, p, p, p, p, *([b_gate.reshape(1, -1)] * N_BRANCH), *([w_branch] * N_BRANCH))


def _rope_tables(n_tok):
    n_rows = n_tok // GRID_W
    row = jnp.arange(n_rows, dtype=F32)[:, None]
    col = jnp.arange(GRID_W, dtype=F32)[:, None]

    def per_token(rows_tab, cols_tab):
        w = rows_tab.shape[1]
        r = jnp.broadcast_to(rows_tab[:, None, :], (n_rows, GRID_W, w)).reshape(n_tok, w)
        c = jnp.broadcast_to(cols_tab[None, :, :], (n_rows, GRID_W, w)).reshape(n_tok, w)
        return r, c

    def table(rot_dim, pad):
        axis_dim = rot_dim // 2
        inv = ROPE_BASE ** (-jnp.arange(0, axis_dim, 2, dtype=F32) / axis_dim)
        ar, ac = row * inv, col * inv
        cos_r, cos_c = per_token(jnp.cos(ar), jnp.cos(ac))
        sin_r, sin_c = per_token(jnp.sin(ar), jnp.sin(ac))
        cos = jnp.concatenate([cos_r, cos_r, cos_c, cos_c], axis=1)
        sin = jnp.concatenate([-sin_r, sin_r, -sin_c, sin_c], axis=1)
        if pad:
            cos = jnp.pad(cos, ((0, 0), (0, pad)))
            sin = jnp.pad(sin, ((0, 0), (0, pad)))
        return cos, sin

    cb, sb = table(GQA_HEAD_DIM, 0)
    cc, sc = table(MLA_ROPE, LANES - MLA_ROPE)
    return cb, sb, cc, sc


def _prep_weights(w_in, w_uq, w_ukv, w_branch, w_out, w_ff2):
    n_l = w_in.shape[0]
    w_main = w_in[:, :, KV_COLS:].astype(BF16)
    w_kv = jnp.pad(w_in[:, :, :KV_COLS], ((0, 0), (0, 0), (0, PK_COLS - KV_COLS)))
    wuq = w_uq.reshape(n_l, MLA_Q_RANK, MLA_HEADS, MLA_NOPE + MLA_ROPE)
    wuq = jnp.pad(wuq, ((0, 0), (0, 0), (0, 0), (0, MLA_QK_PAD - MLA_NOPE - MLA_ROPE)))
    wuq = wuq.reshape(n_l, MLA_Q_RANK, MLA_HEADS * MLA_QK_PAD).astype(BF16)
    wukv = w_ukv.reshape(n_l, MLA_KV_RANK, MLA_HEADS, MLA_NOPE + MLA_V)
    wk = wukv[:, :, :, :MLA_NOPE].reshape(n_l, MLA_KV_RANK, MLA_HEADS * MLA_NOPE).astype(BF16)
    wv = wukv[:, :, :, MLA_NOPE:].reshape(n_l, MLA_KV_RANK, MLA_HEADS * MLA_V).astype(BF16)
    return dict(w_main=w_main, w_kv=w_kv, wuq=wuq, wk=wk, wv=wv,
                w_branch=w_branch.astype(BF16), w_out=w_out.astype(BF16), w_ff2=w_ff2.astype(BF16))


def _kv_of(h, wts, lw, lp, tables):
    p_kv = matmul(h, wts["w_kv"], lw["l"], tn=PK_COLS, name="in_proj_kv")
    return kvprep(p_kv, lp["k_norm_g"], lp["mla_kv_norm_g"], lw["wk"], lw["wv"], tables)


def _mix(h, wts, lw, lp, tables, ctx_kv, x_kv):
    p = matmul(h, wts["w_main"], lw["l"], tm=2048, tn=P_TILE, name="in_proj")
    qbt, qct = qprep(p, lp["q_norm_g"], lp["mla_q_norm_g"], lw["wuq"], tables)
    gqa = lambda kv: None if kv is None else (kv[0], kv[2])
    mla = lambda kv: None if kv is None else (kv[1], kv[3])
    y_b = attention(qbt, gqa(ctx_kv), gqa(x_kv), n_heads=GQA_HEADS, group=GQA_GROUP, dq=GQA_HEAD_DIM,
                    dv=GQA_HEAD_DIM, name="attn_gqa")
    y_c = attention(qct, mla(ctx_kv), mla(x_kv), n_heads=MLA_HEADS, group=1, dq=MLA_QK_PAD, dv=MLA_V,
                    name="attn_mla")
    y_a = short_conv(p, lp["conv_w"])
    y_d = sgu(p, lp["sgu_ln_g"], lp["sgu_ln_b"], lw["sgu_w_s"], lp["sgu_b_s"])
    return merge([y_a, y_b, y_c, y_d], p, lp["b_gate"], lw["w_branch"], lw["l"])


def kernel(x, c, ctx, c_ctx, w_ada, b_ada, norm_mix_g, w_in, b_gate, conv_w, q_norm_g, k_norm_g,
           mla_q_norm_g, mla_kv_norm_g, w_uq, w_ukv, sgu_ln_g, sgu_ln_b, sgu_w_s, sgu_b_s,
           w_branch, w_out, norm_ffn_g, w_ff1, w_ff2, final_norm_g):
    assert x.shape[0] == 1 and c.shape[0] == 1 and ctx.shape[0] == 1
    xs, zs = x[0], ctx[0]
    tables = _rope_tables(xs.shape[0])
    cond_t = jnp.stack([c[0], c_ctx], axis=1)
    X, Z = 0, 1

    wts = _prep_weights(w_in, w_uq, w_ukv, w_branch, w_out, w_ff2)
    b_ada3 = b_ada[:, None, :]

    def mixer_out_and_ffn(merged, s, mod, row, l):
        s, h = matmul(merged, wts["w_out"], l, out_dtype=F32, epilogue="residual_norm", res=s, mod=mod,
                      row=row, k_gate=2, norm_g=norm_ffn_g[l], k_shift=3, k_scale=4,
                      tm=512, tn=D_MODEL, name="out_proj")
        u = matmul(h, w_ff1, l, epilogue="relu2", name="ffn_up")
        return matmul(u, wts["w_ff2"], l, out_dtype=F32, epilogue="residual", res=s, mod=mod, row=row,
                      k_gate=5, tm=1024, tn=256, name="ffn_down")

    for l in range(DEPTH):
        last = l == DEPTH - 1
        lw = dict(l=l, wuq=wts["wuq"][l], wk=wts["wk"][l], wv=wts["wv"][l], w_branch=wts["w_branch"],
                  sgu_w_s=sgu_w_s[l].astype(BF16))
        lp = dict(b_gate=b_gate[l], conv_w=conv_w[l], q_norm_g=q_norm_g[l], k_norm_g=k_norm_g[l],
                  mla_q_norm_g=mla_q_norm_g[l], mla_kv_norm_g=mla_kv_norm_g[l],
                  sgu_ln_g=sgu_ln_g[l], sgu_ln_b=sgu_ln_b[l], sgu_b_s=sgu_b_s[l])
        mod = adaln(cond_t, w_ada, b_ada3, l)

        hz = norm_mod(zs, norm_mix_g[l], mod, row=Z, k_shift=0, k_scale=1)
        kv_z = _kv_of(hz, wts, lw, lp, None)
        hx = norm_mod(xs, norm_mix_g[l], mod, row=X, k_shift=0, k_scale=1)
        kv_x = _kv_of(hx, wts, lw, lp, tables)
        xs = mixer_out_and_ffn(_mix(hx, wts, lw, lp, tables, kv_z, kv_x), xs, mod, X, l)

        if not last:
            zs = mixer_out_and_ffn(_mix(hz, wts, lw, lp, None, kv_z, None), zs, mod, Z, l)

    out = norm_mod(xs, final_norm_g, out_dtype=F32)
    return out[None]
```

```python
import functools
import math

import jax
import jax.numpy as jnp
from jax import lax
from jax.experimental import pallas as pl
from jax.experimental.pallas import tpu as pltpu

F32 = jnp.float32
BF16 = jnp.bfloat16

D_MODEL = 2048
DEPTH = 2
GRID_W = 64
ROPE_BASE = 10000.0
EPS = 1e-6
N_BRANCH = 4
BRANCH_W = 1024
CONV_W = 1024
CONV_K = 3
GQA_HEADS = 8
GQA_KV_HEADS = 2
GQA_GROUP = GQA_HEADS // GQA_KV_HEADS
GQA_HEAD_DIM = 128
GQA_KV_W = GQA_KV_HEADS * GQA_HEAD_DIM
MLA_HEADS = 8
MLA_Q_RANK = 512
MLA_KV_RANK = 512
MLA_NOPE = 128
MLA_ROPE = 64
MLA_V = 128
SGU_W = 1024
SGU_GROUPS = 8
SGU_CHUNK = 128
D_FF = 4 * D_MODEL

KV_COLS = 2 * GQA_KV_W + MLA_KV_RANK + MLA_ROPE
Q_COLS = GQA_HEADS * GQA_HEAD_DIM + MLA_Q_RANK
A_COLS = 3 * CONV_W
D_COLS = 2 * SGU_W
G_COLS = N_BRANCH * D_MODEL

LANES = 128
BF16_SUBLANES = 16
MLA_QK_PAD = 256
ATTN_TK = 512
ATTN_R = 512
ATTN_BLOCKS = 4
ATTN_AHEAD = 3
VMEM_LIMIT_BYTES = 48 * 1024 * 1024

P_TILE = 512
P_QB = 0
P_CQ = P_QB + GQA_HEADS * GQA_HEAD_DIM
P_A = P_CQ + MLA_Q_RANK
P_D = P_A + A_COLS
P_G = P_D + D_COLS
P_COLS = P_G + G_COLS
PK_KB = 0
PK_VB = PK_KB + GQA_KV_W
PK_CKV = PK_VB + GQA_KV_W
PK_KR = PK_CKV + MLA_KV_RANK
PK_COLS = PK_KR + LANES


def _params(*sem):
    return pltpu.CompilerParams(dimension_semantics=sem, vmem_limit_bytes=VMEM_LIMIT_BYTES)


def _adaln_kernel(ct_ref, w_ref, b_ref, o_ref):
    ct = ct_ref[...]
    s = ct / (1.0 + jnp.exp(-ct))
    w = w_ref[...]
    for r in range(2):
        o_ref[r:r + 1, :] = jnp.sum(w * s[:, r:r + 1], axis=0, keepdims=True) + b_ref[...]


def adaln(cond_t, w, b, l, tn=512):
    _, d, n = w.shape
    return pl.pallas_call(
        _adaln_kernel,
        grid=(n // tn,),
        in_specs=[pl.BlockSpec((d, 2), lambda j: (0, 0)),
                  pl.BlockSpec((None, d, tn), lambda j: (l, 0, j)),
                  pl.BlockSpec((None, 1, tn), lambda j: (l, 0, j))],
        out_specs=pl.BlockSpec((2, tn), lambda j: (0, j)),
        out_shape=jax.ShapeDtypeStruct((2, n), F32),
        compiler_params=_params("arbitrary"),
        name="adaln",
    )(cond_t, w, b)


def _norm_kernel(x_ref, g_ref, *rest, row):
    o_ref = rest[-1]
    gain = g_ref[...]
    if len(rest) == 3:
        sh_ref, sc_ref = rest[0], rest[1]
        gain = gain * (1.0 + sc_ref[row:row + 1, :])
        shift = sh_ref[row:row + 1, :]

    def strip(r, carry):
        rows = pl.ds(pl.multiple_of(r * BF16_SUBLANES, BF16_SUBLANES), BF16_SUBLANES)
        x = x_ref[rows, :]
        y = x * lax.rsqrt(jnp.mean(x * x, axis=-1, keepdims=True) + EPS) * gain
        if len(rest) == 3:
            y = y + shift
        o_ref[rows, :] = y.astype(o_ref.dtype)
        return carry

    lax.fori_loop(0, x_ref.shape[0] // BF16_SUBLANES, strip, 0, unroll=4)


def norm_mod(x, g, mod=None, row=0, k_shift=0, k_scale=1, out_dtype=BF16, tm=512):
    m, d = x.shape
    tm = min(tm, m)
    in_specs = [pl.BlockSpec((tm, d), lambda i: (i, 0)), pl.BlockSpec((1, d), lambda i: (0, 0))]
    args = [x, g.reshape(1, d)]
    if mod is not None:
        in_specs += [pl.BlockSpec((2, d), lambda i: (0, k_shift)), pl.BlockSpec((2, d), lambda i: (0, k_scale))]
        args += [mod, mod]
    return pl.pallas_call(
        functools.partial(_norm_kernel, row=row),
        grid=(m // tm,),
        in_specs=in_specs,
        out_specs=pl.BlockSpec((tm, d), lambda i: (i, 0)),
        out_shape=jax.ShapeDtypeStruct((m, d), out_dtype),
        compiler_params=_params("parallel"),
        name="norm_mod",
    )(*args)


def _mm_kernel(a_ref, w_ref, *rest, epilogue, row, cast_w):
    if cast_w:
        wb_ref = rest[-1]
        rest = rest[:-1]

        @pl.when(pl.program_id(1) == 0)
        def _():
            wb_ref[...] = w_ref[...].astype(BF16)

        w = wb_ref[...]
    else:
        w = w_ref[...]
    acc = jnp.dot(a_ref[...], w, preferred_element_type=F32)
    if epilogue == "residual_norm":
        res_ref, gate_ref, g_ref, sh_ref, sc_ref, o_ref, h_ref = rest
        x = res_ref[...] + gate_ref[row:row + 1, :] * acc
        o_ref[...] = x
        gain = g_ref[...] * (1.0 + sc_ref[row:row + 1, :])
        y = x * lax.rsqrt(jnp.mean(x * x, axis=-1, keepdims=True) + EPS) * gain + sh_ref[row:row + 1, :]
        h_ref[...] = y.astype(h_ref.dtype)
        return
    o_ref = rest[-1]
    if epilogue == "relu2":
        r = jnp.maximum(acc, 0.0)
        acc = r * r
    elif epilogue == "residual":
        res_ref, gate_ref = rest[0], rest[1]
        acc = res_ref[...] + gate_ref[row:row + 1, :] * acc
    o_ref[...] = acc.astype(o_ref.dtype)


def matmul(a, w, l, *, out_dtype=BF16, epilogue="none", res=None, mod=None, row=0, k_gate=0,
           norm_g=None, k_shift=0, k_scale=0, tm=1024, tn=1024, name="matmul"):
    m, k = a.shape
    n = w.shape[2]
    tm, tn = min(tm, m), min(tn, n)
    assert m % tm == 0 and n % tn == 0
    cast_w = w.dtype == F32
    fused_norm = epilogue == "residual_norm"
    assert not fused_norm or (tn == n and not cast_w)
    order = (lambda f: (lambda j, i: f(i, j))) if cast_w else (lambda f: f)
    in_specs = [pl.BlockSpec((tm, k), order(lambda i, j: (i, 0))),
                pl.BlockSpec((None, k, tn), order(lambda i, j: (l, 0, j)))]
    args = [a, w]
    if epilogue in ("residual", "residual_norm"):
        kb = k_gate * (n // tn)
        in_specs += [pl.BlockSpec((tm, tn), order(lambda i, j: (i, j))),
                     pl.BlockSpec((2, tn), order(lambda i, j: (0, kb + j)))]
        args += [res, mod]
    out_specs = pl.BlockSpec((tm, tn), order(lambda i, j: (i, j)))
    out_shape = jax.ShapeDtypeStruct((m, n), out_dtype)
    if fused_norm:
        in_specs += [pl.BlockSpec((1, n), lambda i, j: (0, 0)),
                     pl.BlockSpec((2, n), lambda i, j: (0, k_shift)),
                     pl.BlockSpec((2, n), lambda i, j: (0, k_scale))]
        args += [norm_g.reshape(1, n), mod, mod]
        out_specs = [out_specs, out_specs]
        out_shape = [out_shape, jax.ShapeDtypeStruct((m, n), BF16)]
    grid = (n // tn, m // tm) if cast_w else (m // tm, n // tn)
    return pl.pallas_call(
        functools.partial(_mm_kernel, epilogue=epilogue, row=row, cast_w=cast_w),
        grid=grid,
        in_specs=in_specs,
        out_specs=out_specs,
        out_shape=out_shape,
        scratch_shapes=[pltpu.VMEM((k, tn), BF16)] if cast_w else [],
        compiler_params=_params("parallel", "arbitrary"),
        name=name,
    )(*args)


def _rope(y, cos, sin, seg):
    lane = lax.broadcasted_iota(jnp.int32, y.shape, 1)
    first = (lane // seg) % 2 == 0
    partner = jnp.where(first, pltpu.roll(y, LANES - seg, 1), pltpu.roll(y, seg, 1))
    return y * cos + partner * sin


def _transpose_tiles(x):
    i0 = lax.broadcasted_iota(jnp.int32, (LANES, LANES), 0)
    i1 = lax.broadcasted_iota(jnp.int32, (LANES, LANES), 1)
    eye = jnp.where(i0 == i1, 1.0, 0.0).astype(x.dtype)
    tiles = [lax.dot_general(eye, x[:, k:k + LANES], (((1,), (1,)), ((), ())), preferred_element_type=F32)
             for k in range(0, x.shape[1], LANES)]
    return jnp.concatenate(tiles, axis=0).astype(x.dtype)


def _head_rms(x, g):
    return x * lax.rsqrt(jnp.mean(x * x, axis=-1, keepdims=True) + EPS) * g


LOG2_E = math.log2(math.e)
GQA_SEG = GQA_HEAD_DIM // 4
MLA_SEG = MLA_ROPE // 4


def _qprep_kernel(qb_ref, cq_ref, qg_ref, cqg_ref, wuq_ref, *rest, rope):
    if rope:
        cb_ref, sb_ref, cc_ref, sc_ref, oqb_ref, oqc_ref = rest
    else:
        oqb_ref, oqc_ref = rest
    qg = qg_ref[...]
    scale_b = LOG2_E * GQA_HEAD_DIM ** -0.5
    for h in range(GQA_HEADS):
        sl = slice(h * GQA_HEAD_DIM, (h + 1) * GQA_HEAD_DIM)
        y = _head_rms(qb_ref[:, sl].astype(F32), qg)
        if rope:
            y = _rope(y, cb_ref[...], sb_ref[...], GQA_SEG)
        oqb_ref[sl, :] = _transpose_tiles((y * scale_b).astype(oqb_ref.dtype))
    cq = _head_rms(cq_ref[...].astype(F32), cqg_ref[...]).astype(BF16)
    qc = jnp.dot(cq, wuq_ref[...], preferred_element_type=F32)
    scale_c = LOG2_E * (MLA_NOPE + MLA_ROPE) ** -0.5
    for h in range(MLA_HEADS):
        nope = slice(h * MLA_QK_PAD, h * MLA_QK_PAD + MLA_NOPE)
        rot = slice(h * MLA_QK_PAD + MLA_NOPE, (h + 1) * MLA_QK_PAD)
        oqc_ref[nope, :] = _transpose_tiles((qc[:, nope] * scale_c).astype(oqc_ref.dtype))
        y = qc[:, rot]
        if rope:
            y = _rope(y, cc_ref[...], sc_ref[...], MLA_SEG)
        oqc_ref[rot, :] = _transpose_tiles((y * scale_c).astype(oqc_ref.dtype))


def qprep(p, qg, cqg, wuq, tables, tm=256):
    m = p.shape[0]
    tm = min(tm, m)
    wq = GQA_HEADS * GQA_HEAD_DIM
    qb_blk = P_QB // wq
    cq_blk = P_CQ // MLA_Q_RANK
    assert P_QB % wq == 0 and P_CQ % MLA_Q_RANK == 0
    in_specs = [pl.BlockSpec((tm, wq), lambda i: (i, qb_blk)),
                pl.BlockSpec((tm, MLA_Q_RANK), lambda i: (i, cq_blk)),
                pl.BlockSpec((1, GQA_HEAD_DIM), lambda i: (0, 0)),
                pl.BlockSpec((1, MLA_Q_RANK), lambda i: (0, 0)),
                pl.BlockSpec(wuq.shape, lambda i: (0, 0))]
    args = [p, p, qg.reshape(1, -1), cqg.reshape(1, -1), wuq]
    if tables is not None:
        in_specs += [pl.BlockSpec((tm, LANES), lambda i: (i, 0))] * 4
        args += list(tables)
    wc = MLA_HEADS * MLA_QK_PAD
    return pl.pallas_call(
        functools.partial(_qprep_kernel, rope=tables is not None),
        grid=(m // tm,),
        in_specs=in_specs,
        out_specs=[pl.BlockSpec((wq, tm), lambda i: (0, i)), pl.BlockSpec((wc, tm), lambda i: (0, i))],
        out_shape=[jax.ShapeDtypeStruct((wq, m), BF16), jax.ShapeDtypeStruct((wc, m), BF16)],
        compiler_params=_params("parallel"),
        name="qprep",
    )(*args)


def _kvprep_kernel(kb_ref, vb_ref, ckv_ref, kr_ref, kg_ref, ckvg_ref, wk_ref, wv_ref, *rest, rope):
    if rope:
        cb_ref, sb_ref, cc_ref, sc_ref, okb_ref, okc_ref, ovb_ref, ovc_ref = rest
    else:
        okb_ref, okc_ref, ovb_ref, ovc_ref = rest
    kg = kg_ref[...]
    for h in range(GQA_KV_HEADS):
        sl = slice(h * GQA_HEAD_DIM, (h + 1) * GQA_HEAD_DIM)
        y = _head_rms(kb_ref[:, sl].astype(F32), kg)
        if rope:
            y = _rope(y, cb_ref[...], sb_ref[...], GQA_SEG)
        okb_ref[:, sl] = y.astype(okb_ref.dtype)
    ovb_ref[...] = _transpose_tiles(vb_ref[...])
    ckv = _head_rms(ckv_ref[...].astype(F32), ckvg_ref[...]).astype(BF16)
    knope = jnp.dot(ckv, wk_ref[...], preferred_element_type=F32)
    ovc_ref[...] = _transpose_tiles(jnp.dot(ckv, wv_ref[...], preferred_element_type=F32).astype(ovc_ref.dtype))
    kr = kr_ref[...].astype(F32)
    if rope:
        kr = _rope(kr, cc_ref[...], sc_ref[...], MLA_SEG)
    kr = kr.astype(okc_ref.dtype)
    for h in range(MLA_HEADS):
        okc_ref[:, h * MLA_QK_PAD:h * MLA_QK_PAD + MLA_NOPE] = (
            knope[:, h * MLA_NOPE:(h + 1) * MLA_NOPE].astype(okc_ref.dtype))
        okc_ref[:, h * MLA_QK_PAD + MLA_NOPE:(h + 1) * MLA_QK_PAD] = kr


def kvprep(p, kg, ckvg, wk, wv, tables):
    m = p.shape[0]
    tm = min(ATTN_TK, m)
    kb_blk = PK_KB // GQA_KV_W
    vb_blk = PK_VB // GQA_KV_W
    ckv_blk = PK_CKV // MLA_KV_RANK
    kr_blk = PK_KR // LANES
    assert PK_KB % GQA_KV_W == 0 and PK_VB % GQA_KV_W == 0 and PK_CKV % MLA_KV_RANK == 0 and PK_KR % LANES == 0
    in_specs = [pl.BlockSpec((tm, GQA_KV_W), lambda i: (i, kb_blk)),
                pl.BlockSpec((tm, GQA_KV_W), lambda i: (i, vb_blk)),
                pl.BlockSpec((tm, MLA_KV_RANK), lambda i: (i, ckv_blk)),
                pl.BlockSpec((tm, LANES), lambda i: (i, kr_blk)),
                pl.BlockSpec((1, GQA_HEAD_DIM), lambda i: (0, 0)),
                pl.BlockSpec((1, MLA_KV_RANK), lambda i: (0, 0)),
                pl.BlockSpec(wk.shape, lambda i: (0, 0)),
                pl.BlockSpec(wv.shape, lambda i: (0, 0))]
    args = [p, p, p, p, kg.reshape(1, -1), ckvg.reshape(1, -1), wk, wv]
    if tables is not None:
        in_specs += [pl.BlockSpec((tm, LANES), lambda i: (i, 0))] * 4
        args += list(tables)
    wkc = MLA_HEADS * MLA_QK_PAD
    wvc = MLA_HEADS * MLA_V
    n = m // tm
    return pl.pallas_call(
        functools.partial(_kvprep_kernel, rope=tables is not None),
        grid=(n,),
        in_specs=in_specs,
        out_specs=[pl.BlockSpec((tm, GQA_KV_W), lambda i: (i, 0)),
                   pl.BlockSpec((tm, wkc), lambda i: (i, 0)),
                   pl.BlockSpec((None, GQA_KV_W, tm), lambda i: (i, 0, 0)),
                   pl.BlockSpec((None, wvc, tm), lambda i: (i, 0, 0))],
        out_shape=[jax.ShapeDtypeStruct((m, GQA_KV_W), BF16),
                   jax.ShapeDtypeStruct((m, wkc), BF16),
                   jax.ShapeDtypeStruct((n, GQA_KV_W, tm), BF16),
                   jax.ShapeDtypeStruct((n, wvc, tm), BF16)],
        compiler_params=_params("parallel"),
        name="kvprep",
    )(*args)


def _attn_kernel(*refs, has_x):
    if has_x:
        qt_ref, kc_ref, vct_ref, kx_ref, vxt_ref, o_ref, s_ref = refs
        n, _, tk = vxt_ref.shape
    else:
        qt_ref, kc_ref, vct_ref, o_ref, s_ref = refs
        n, tk = 0, 0
    n_slot, _, r = s_ref.shape
    rows = [kc_ref.shape[0]] + [tk] * n
    items = [(b, g) for b in range(qt_ref.shape[1] // r) for g in range(n + 1)]

    def keys(g):
        return kc_ref[...] if g == 0 else kx_ref[(g - 1) * tk:g * tk, :]

    def values_t(g):
        return vct_ref[...] if g == 0 else vxt_ref[g - 1]

    def qk(t):
        b, g = items[t]
        st = jnp.dot(keys(g), qt_ref[:, b * r:(b + 1) * r], preferred_element_type=F32)
        s_ref[t % n_slot, :rows[g], :] = st
        return jnp.max(st, axis=0, keepdims=True)

    chunk_max = {t: qk(t) for t in range(min(ATTN_AHEAD, len(items)))}
    m = l = acc = None
    for t, (b, g) in enumerate(items):
        if t + ATTN_AHEAD < len(items):
            chunk_max[t + ATTN_AHEAD] = qk(t + ATTN_AHEAD)
        m_new = chunk_max.pop(t) if g == 0 else jnp.maximum(m, chunk_max.pop(t))
        pt = jnp.exp2(s_ref[t % n_slot, :rows[g], :] - m_new)
        p_sum = jnp.sum(pt, axis=0, keepdims=True)
        pv = jnp.dot(values_t(g), pt.astype(BF16), preferred_element_type=F32)
        if g == 0:
            l, acc = p_sum, pv
        else:
            alpha = jnp.exp2(m - m_new)
            l, acc = alpha * l + p_sum, alpha * acc + pv
        m = m_new
        if g == n:
            o_ref[b * r:(b + 1) * r, :] = (acc / l).T.astype(o_ref.dtype)


def attention(qt, ctx_kv, x_kv, *, n_heads, group, dq, dv, name="attn"):
    s_len = qt.shape[1]
    r = min(ATTN_R, s_len)
    tq = min(ATTN_BLOCKS * r, s_len)
    kc, vct = ctx_kv
    tc = kc.shape[0]
    in_specs = [pl.BlockSpec((dq, tq), lambda h, i: (h, i)),
                pl.BlockSpec((tc, dq), lambda h, i: (0, h // group)),
                pl.BlockSpec((None, dv, tc), lambda h, i: (0, h // group, 0))]
    args = [qt, kc, vct]
    n_chunk, rows = 1, tc
    if x_kv is not None:
        kx, vxt = x_kv
        in_specs += [pl.BlockSpec((kx.shape[0], dq), lambda h, i: (0, h // group)),
                     pl.BlockSpec((vxt.shape[0], dv, vxt.shape[2]), lambda h, i: (0, h // group, 0))]
        args += [kx, vxt]
        n_chunk, rows = 1 + vxt.shape[0], max(tc, vxt.shape[2])
    return pl.pallas_call(
        functools.partial(_attn_kernel, has_x=x_kv is not None),
        grid=(n_heads, s_len // tq),
        in_specs=in_specs,
        out_specs=pl.BlockSpec((tq, dv), lambda h, i: (i, h)),
        out_shape=jax.ShapeDtypeStruct((s_len, n_heads * dv), BF16),
        scratch_shapes=[pltpu.VMEM((min(ATTN_AHEAD + 1, n_chunk), rows, r), F32)],
        compiler_params=_params("parallel", "arbitrary"),
        name=name,
    )(*args)


def _conv_kernel(gb_ref, gc_ref, xa_ref, gcp_ref, xap_ref, gcn_ref, xan_ref, w_ref, o_ref):
    i = pl.program_id(0)
    tm = gb_ref.shape[0]
    u = gc_ref[...].astype(F32) * xa_ref[...].astype(F32)
    halo = gcp_ref.shape[0]
    u_prev = gcp_ref[halo - 1:halo, :].astype(F32) * xap_ref[halo - 1:halo, :].astype(F32)
    u_next = gcn_ref[0:1, :].astype(F32) * xan_ref[0:1, :].astype(F32)
    u_prev = jnp.where(i == 0, 0.0, u_prev)
    u_next = jnp.where(i == pl.num_programs(0) - 1, 0.0, u_next)
    row = lax.broadcasted_iota(jnp.int32, u.shape, 0)
    below = jnp.where(row == 0, u_prev, pltpu.roll(u, 1, 0))
    above = jnp.where(row == tm - 1, u_next, pltpu.roll(u, tm - 1, 0))
    w = w_ref[...]
    y = below * w[0:1, :] + u * w[1:2, :] + above * w[2:3, :]
    o_ref[...] = (gb_ref[...].astype(F32) * y).astype(o_ref.dtype)


def short_conv(p, w, tm=1024):
    m = p.shape[0]
    tm = min(tm, m)
    c = P_TILE
    nc = CONV_W // c
    blk = P_A // c
    assert P_A % c == 0 and CONV_W % c == 0
    hb = tm // BF16_SUBLANES
    n_hb = m // BF16_SUBLANES
    main = lambda k: pl.BlockSpec((tm, c), lambda i, j: (i, blk + k * nc + j))
    prev = lambda k: pl.BlockSpec((BF16_SUBLANES, c),
                                  lambda i, j: (jnp.maximum(i * hb - 1, 0), blk + k * nc + j))
    nxt = lambda k: pl.BlockSpec((BF16_SUBLANES, c),
                                 lambda i, j: (jnp.minimum((i + 1) * hb, n_hb - 1), blk + k * nc + j))
    return pl.pallas_call(
        _conv_kernel,
        grid=(m // tm, nc),
        in_specs=[main(0), main(1), main(2), prev(1), prev(2), nxt(1), nxt(2),
                  pl.BlockSpec((CONV_K, c), lambda i, j: (0, j))],
        out_specs=pl.BlockSpec((tm, c), lambda i, j: (i, j)),
        out_shape=jax.ShapeDtypeStruct((m, CONV_W), BF16),
        compiler_params=_params("parallel", "parallel"),
        name="short_conv",
    )(p, p, p, p, p, p, p, w)


def _gelu_tanh(x):
    c = math.sqrt(2.0 / math.pi)
    return x * (0.5 * (1.0 + jnp.tanh(c * (x + 0.044715 * (x * x * x)))))


def _sgu_kernel(*refs):
    nb = SGU_W // P_TILE
    u_refs, v_refs = refs[:nb], refs[nb:2 * nb]
    g_ref, b_ref, ws_ref, bs_ref, o_ref = refs[2 * nb:]
    tm = o_ref.shape[0]
    n_chunk = tm // SGU_CHUNK
    gw = SGU_W // SGU_GROUPS
    v = _gelu_tanh(jnp.concatenate([r[...] for r in v_refs], axis=1).astype(F32))
    mu = jnp.mean(v, axis=-1, keepdims=True)
    vc = v - mu
    vn = vc * lax.rsqrt(jnp.mean(vc * vc, axis=-1, keepdims=True) + EPS) * g_ref[...] + b_ref[...]
    vn = vn.astype(BF16)
    for g in range(SGU_GROUPS):
        cols = slice(g * gw, (g + 1) * gw)
        rhs = jnp.concatenate([vn[c * SGU_CHUNK:(c + 1) * SGU_CHUNK, cols] for c in range(n_chunk)], axis=1)
        mixed = jnp.dot(ws_ref[g], rhs, preferred_element_type=F32)
        for c in range(n_chunk):
            rows = slice(c * SGU_CHUNK, (c + 1) * SGU_CHUNK)
            u_ref, u0 = u_refs[g * gw // P_TILE], g * gw % P_TILE
            u = _gelu_tanh(u_ref[rows, u0:u0 + gw].astype(F32))
            o_ref[rows, cols] = (u * (mixed[:, c * gw:(c + 1) * gw] + bs_ref[g])).astype(o_ref.dtype)


def sgu(p, ln_g, ln_b, w_s, b_s, tm=512):
    m = p.shape[0]
    tm = min(tm, m)
    c = SGU_W
    nb = c // P_TILE
    blk = P_D // P_TILE
    assert P_D % P_TILE == 0 and c % P_TILE == 0
    gw = SGU_W // SGU_GROUPS
    bs = jnp.broadcast_to(b_s[:, :, None], (SGU_GROUPS, SGU_CHUNK, gw))
    uv_specs = [pl.BlockSpec((tm, P_TILE), lambda i, k=k: (i, blk + k)) for k in range(2 * nb)]
    return pl.pallas_call(
        _sgu_kernel,
        grid=(m // tm,),
        in_specs=uv_specs + [
                  pl.BlockSpec((1, c), lambda i: (0, 0)),
                  pl.BlockSpec((1, c), lambda i: (0, 0)),
                  pl.BlockSpec(w_s.shape, lambda i: (0, 0, 0)),
                  pl.BlockSpec(bs.shape, lambda i: (0, 0, 0))],
        out_specs=pl.BlockSpec((tm, c), lambda i: (i, 0)),
        out_shape=jax.ShapeDtypeStruct((m, c), BF16),
        compiler_params=_params("parallel"),
        name="sgu",
    )(*([p] * (2 * nb)), ln_g.reshape(1, c), ln_b.reshape(1, c), w_s, bs)


def _merge_kernel(*refs):
    ys, pgs, bgs, wbs, o_ref = refs[0:4], refs[4:8], refs[8:12], refs[12:16], refs[16]
    acc = None
    for i in range(N_BRANCH):
        z = pgs[i][...].astype(F32) + bgs[i][...]
        gate = 1.0 / (1.0 + jnp.exp(-z))
        term = gate * jnp.dot(ys[i][...], wbs[i][...], preferred_element_type=F32)
        acc = term if acc is None else acc + term
    o_ref[...] = acc.astype(o_ref.dtype)


def merge(ys, p, b_gate, w_branch, l, tm=1024, tn=P_TILE):
    m = p.shape[0]
    tm = min(tm, m)
    nj = D_MODEL // tn
    g_blk = P_G // tn
    assert P_G % tn == 0
    in_specs = [pl.BlockSpec((tm, BRANCH_W), lambda i, j: (i, 0))] * N_BRANCH
    in_specs += [pl.BlockSpec((tm, tn), lambda i, j, b=b: (i, g_blk + b * nj + j)) for b in range(N_BRANCH)]
    in_specs += [pl.BlockSpec((1, tn), lambda i, j, b=b: (0, b * nj + j)) for b in range(N_BRANCH)]
    in_specs += [pl.BlockSpec((None, None, BRANCH_W, tn), lambda i, j, b=b: (l, b, 0, j))
                 for b in range(N_BRANCH)]
    return pl.pallas_call(
        _merge_kernel,
        grid=(m // tm, nj),
        in_specs=in_specs,
        out_specs=pl.BlockSpec((tm, tn), lambda i, j: (i, j)),
        out_shape=jax.ShapeDtypeStruct((m, D_MODEL), BF16),
        compiler_params=_params("parallel", "arbitrary"),
        name="merge",
    )(*ys, p, p, p, p, *([b_gate.reshape(1, -1)] * N_BRANCH), *([w_branch] * N_BRANCH))


def _rope_tables(n_tok):
    n_rows = n_tok // GRID_W
    row = jnp.arange(n_rows, dtype=F32)[:, None]
    col = jnp.arange(GRID_W, dtype=F32)[:, None]

    def per_token(rows_tab, cols_tab):
        w = rows_tab.shape[1]
        r = jnp.broadcast_to(rows_tab[:, None, :], (n_rows, GRID_W, w)).reshape(n_tok, w)
        c = jnp.broadcast_to(cols_tab[None, :, :], (n_rows, GRID_W, w)).reshape(n_tok, w)
        return r, c

    def table(rot_dim, pad):
        axis_dim = rot_dim // 2
        inv = ROPE_BASE ** (-jnp.arange(0, axis_dim, 2, dtype=F32) / axis_dim)
        ar, ac = row * inv, col * inv
        cos_r, cos_c = per_token(jnp.cos(ar), jnp.cos(ac))
        sin_r, sin_c = per_token(jnp.sin(ar), jnp.sin(ac))
        cos = jnp.concatenate([cos_r, cos_r, cos_c, cos_c], axis=1)
        sin = jnp.concatenate([-sin_r, sin_r, -sin_c, sin_c], axis=1)
        if pad:
            cos = jnp.pad(cos, ((0, 0), (0, pad)))
            sin = jnp.pad(sin, ((0, 0), (0, pad)))
        return cos, sin

    cb, sb = table(GQA_HEAD_DIM, 0)
    cc, sc = table(MLA_ROPE, LANES - MLA_ROPE)
    return cb, sb, cc, sc


def _prep_weights(w_in, w_uq, w_ukv, w_branch, w_out, w_ff2):
    n_l = w_in.shape[0]
    w_main = w_in[:, :, KV_COLS:].astype(BF16)
    w_kv = jnp.pad(w_in[:, :, :KV_COLS], ((0, 0), (0, 0), (0, PK_COLS - KV_COLS)))
    wuq = w_uq.reshape(n_l, MLA_Q_RANK, MLA_HEADS, MLA_NOPE + MLA_ROPE)
    wuq = jnp.pad(wuq, ((0, 0), (0, 0), (0, 0), (0, MLA_QK_PAD - MLA_NOPE - MLA_ROPE)))
    wuq = wuq.reshape(n_l, MLA_Q_RANK, MLA_HEADS * MLA_QK_PAD).astype(BF16)
    wukv = w_ukv.reshape(n_l, MLA_KV_RANK, MLA_HEADS, MLA_NOPE + MLA_V)
    wk = wukv[:, :, :, :MLA_NOPE].reshape(n_l, MLA_KV_RANK, MLA_HEADS * MLA_NOPE).astype(BF16)
    wv = wukv[:, :, :, MLA_NOPE:].reshape(n_l, MLA_KV_RANK, MLA_HEADS * MLA_V).astype(BF16)
    return dict(w_main=w_main, w_kv=w_kv, wuq=wuq, wk=wk, wv=wv,
                w_branch=w_branch.astype(BF16), w_out=w_out.astype(BF16), w_ff2=w_ff2.astype(BF16))


def _kv_of(h, wts, lw, lp, tables):
    p_kv = matmul(h, wts["w_kv"], lw["l"], tn=PK_COLS, name="in_proj_kv")
    return kvprep(p_kv, lp["k_norm_g"], lp["mla_kv_norm_g"], lw["wk"], lw["wv"], tables)


def _mix(h, wts, lw, lp, tables, ctx_kv, x_kv):
    p = matmul(h, wts["w_main"], lw["l"], tm=2048, tn=P_TILE, name="in_proj")
    qbt, qct = qprep(p, lp["q_norm_g"], lp["mla_q_norm_g"], lw["wuq"], tables)
    gqa = lambda kv: None if kv is None else (kv[0], kv[2])
    mla = lambda kv: None if kv is None else (kv[1], kv[3])
    y_b = attention(qbt, gqa(ctx_kv), gqa(x_kv), n_heads=GQA_HEADS, group=GQA_GROUP, dq=GQA_HEAD_DIM,
                    dv=GQA_HEAD_DIM, name="attn_gqa")
    y_c = attention(qct, mla(ctx_kv), mla(x_kv), n_heads=MLA_HEADS, group=1, dq=MLA_QK_PAD, dv=MLA_V,
                    name="attn_mla")
    y_a = short_conv(p, lp["conv_w"])
    y_d = sgu(p, lp["sgu_ln_g"], lp["sgu_ln_b"], lw["sgu_w_s"], lp["sgu_b_s"])
    return merge([y_a, y_b, y_c, y_d], p, lp["b_gate"], lw["w_branch"], lw["l"])


def kernel(x, c, ctx, c_ctx, w_ada, b_ada, norm_mix_g, w_in, b_gate, conv_w, q_norm_g, k_norm_g,
           mla_q_norm_g, mla_kv_norm_g, w_uq, w_ukv, sgu_ln_g, sgu_ln_b, sgu_w_s, sgu_b_s,
           w_branch, w_out, norm_ffn_g, w_ff1, w_ff2, final_norm_g):
    assert x.shape[0] == 1 and c.shape[0] == 1 and ctx.shape[0] == 1
    xs, zs = x[0], ctx[0]
    tables = _rope_tables(xs.shape[0])
    cond_t = jnp.stack([c[0], c_ctx], axis=1)
    X, Z = 0, 1

    wts = _prep_weights(w_in, w_uq, w_ukv, w_branch, w_out, w_ff2)
    b_ada3 = b_ada[:, None, :]

    def mixer_out_and_ffn(merged, s, mod, row, l):
        s, h = matmul(merged, wts["w_out"], l, out_dtype=F32, epilogue="residual_norm", res=s, mod=mod,
                      row=row, k_gate=2, norm_g=norm_ffn_g[l], k_shift=3, k_scale=4,
                      tm=512, tn=D_MODEL, name="out_proj")
        u = matmul(h, w_ff1, l, epilogue="relu2", name="ffn_up")
        return matmul(u, wts["w_ff2"], l, out_dtype=F32, epilogue="residual", res=s, mod=mod, row=row,
                      k_gate=5, tm=1024, tn=256, name="ffn_down")

    for l in range(DEPTH):
        last = l == DEPTH - 1
        lw = dict(l=l, wuq=wts["wuq"][l], wk=wts["wk"][l], wv=wts["wv"][l], w_branch=wts["w_branch"],
                  sgu_w_s=sgu_w_s[l].astype(BF16))
        lp = dict(b_gate=b_gate[l], conv_w=conv_w[l], q_norm_g=q_norm_g[l], k_norm_g=k_norm_g[l],
                  mla_q_norm_g=mla_q_norm_g[l], mla_kv_norm_g=mla_kv_norm_g[l],
                  sgu_ln_g=sgu_ln_g[l], sgu_ln_b=sgu_ln_b[l], sgu_b_s=sgu_b_s[l])
        mod = adaln(cond_t, w_ada, b_ada3, l)

        hz = norm_mod(zs, norm_mix_g[l], mod, row=Z, k_shift=0, k_scale=1)
        kv_z = _kv_of(hz, wts, lw, lp, None)
        hx = norm_mod(xs, norm_mix_g[l], mod, row=X, k_shift=0, k_scale=1)
        kv_x = _kv_of(hx, wts, lw, lp, tables)
        xs = mixer_out_and_ffn(_mix(hx, wts, lw, lp, tables, kv_z, kv_x), xs, mod, X, l)

        if not last:
            zs = mixer_out_and_ffn(_mix(hz, wts, lw, lp, None, kv_z, None), zs, mod, Z, l)

    out = norm_mod(xs, final_norm_g, out_dtype=F32)
    return out[None]
```
